```python
import jax
import jax.numpy as jnp
from jax import lax

D_MODEL = 1024
BATCH = 8
SEQ = 2048
DEPTH = 4

N_ATTN_HEADS = 8
HEAD_DIM = 64
ATTN_WIDTH = N_ATTN_HEADS * HEAD_DIM
MOBA_BLOCK = 256
MOBA_TOPK = 3
MOBA_Q_BLOCK = 64
N_SGU_GROUPS = 8
SGU_GROUP_DIM = 64
SGU_WIDTH = N_SGU_GROUPS * SGU_GROUP_DIM
SGU_CHUNK = 128
D_FF = 2816
CONV_WIDTH = 3
N_MOD = 6
EPS = 1e-6
IN_WIDTH = 3 * ATTN_WIDTH + 2 * SGU_WIDTH + 2 * D_MODEL

kernel_name = 'hybrid_moba_gmlp_convffn_adaln'


def rms_norm(x, g):
    xf = x.astype(jnp.float32)
    y = xf * lax.rsqrt(jnp.mean(xf * xf, axis=-1, keepdims=True) + EPS)
    return (y * g.astype(jnp.float32)).astype(x.dtype)


def modulate(h, shift, scale):
    return h * (1 + scale[:, None, :]) + shift[:, None, :]


def moba_attention(q, k, v):
    b, s, h, dh = q.shape
    s_pad = -(-s // MOBA_BLOCK) * MOBA_BLOCK
    pad = ((0, 0), (0, s_pad - s), (0, 0), (0, 0))
    nb = s_pad // MOBA_BLOCK
    q = jnp.pad(q * dh ** -0.5, pad).transpose(0, 2, 1, 3)
    kb = jnp.pad(k, pad).transpose(0, 2, 1, 3).reshape(b, h, nb, MOBA_BLOCK, dh)
    vb = jnp.pad(v, pad).transpose(0, 2, 1, 3).reshape(b, h, nb, MOBA_BLOCK, dh)
    topk = min(MOBA_TOPK, nb)
    k_mean = jnp.mean(kb.astype(jnp.float32), axis=3)
    route = jnp.einsum('bhsd,bhnd->bhsn', q.astype(jnp.float32), k_mean)
    q_block = jnp.arange(s_pad) // MOBA_BLOCK
    fully_past = jnp.arange(nb)[None, :] < q_block[:, None]
    route = jnp.where(fully_past, route, -jnp.inf)
    _, sel = lax.top_k(route, topk)
    nq = s_pad // MOBA_Q_BLOCK
    q_c = q.reshape(b, h, nq, MOBA_Q_BLOCK, dh).transpose(2, 0, 1, 3, 4)
    sel_c = sel.reshape(b, h, nq, MOBA_Q_BLOCK, topk).transpose(2, 0, 1, 3, 4)
    bi = jnp.arange(b)[:, None, None, None]
    hi = jnp.arange(h)[None, :, None, None]

    def query_block(args):
        ci, qc, selc = args
        pos_q = ci * MOBA_Q_BLOCK + jnp.arange(MOBA_Q_BLOCK)
        blk = (ci * MOBA_Q_BLOCK) // MOBA_BLOCK
        k_sel = kb[bi, hi, selc]
        v_sel = vb[bi, hi, selc]
        s_sel = jnp.einsum('bhqd,bhqnkd->bhqnk', qc, k_sel).astype(jnp.float32)
        valid = jnp.arange(topk)[None, :] < (pos_q // MOBA_BLOCK)[:, None]
        s_sel = jnp.where(valid[None, None, :, :, None], s_sel, -jnp.inf)
        k_own = lax.dynamic_index_in_dim(kb, blk, axis=2, keepdims=False)
        v_own = lax.dynamic_index_in_dim(vb, blk, axis=2, keepdims=False)
        s_own = jnp.einsum('bhqd,bhkd->bhqk', qc, k_own).astype(jnp.float32)
        pos_k = blk * MOBA_BLOCK + jnp.arange(MOBA_BLOCK)
        s_own = jnp.where(pos_k[None, :] <= pos_q[:, None], s_own, -jnp.inf)
        scores = jnp.concatenate(
            [s_sel.reshape(b, h, MOBA_Q_BLOCK, topk * MOBA_BLOCK), s_own], axis=-1)
        p = jax.nn.softmax(scores, axis=-1).astype(v.dtype)
        p_sel = p[..., :topk * MOBA_BLOCK].reshape(b, h, MOBA_Q_BLOCK, topk, MOBA_BLOCK)
        p_own = p[..., topk * MOBA_BLOCK:]
        return (jnp.einsum('bhqnk,bhqnkd->bhqd', p_sel, v_sel)
                + jnp.einsum('bhqk,bhkd->bhqd', p_own, v_own))

    out = lax.map(query_block, (jnp.arange(nq), q_c, sel_c))
    out = out.transpose(1, 0, 3, 2, 4).reshape(b, s_pad, h * dh)
    return out[:, :s]


def spatial_gating(u, v, g_v, w_s, b_s):
    b, s, _ = v.shape
    v = rms_norm(v, g_v).reshape(b, s // SGU_CHUNK, SGU_CHUNK, N_SGU_GROUPS, SGU_GROUP_DIM)
    causal = jnp.tril(jnp.ones((SGU_CHUNK, SGU_CHUNK), dtype=bool))
    w = jnp.where(causal[None], w_s, 0)
    mixed = jnp.einsum('gij,bnjgc->bnigc', w, v) + b_s.T[None, None, :, :, None]
    return u * mixed.reshape(b, s, SGU_WIDTH)


def causal_dwconv(h, w, bias):
    s = h.shape[1]
    hp = jnp.pad(h, ((0, 0), (CONV_WIDTH - 1, 0), (0, 0)))
    out = bias + w[CONV_WIDTH - 1] * h
    for i in range(CONV_WIDTH - 1):
        out = out + w[i] * hp[:, i:i + s]
    return out


def setup_inputs(seed: int = 0) -> dict:
    key = jax.random.key(seed)
    ks = jax.random.split(key, 18)

    def nrm(k, shape, scale):
        return jax.random.normal(k, shape, jnp.float32) * scale

    L = DEPTH
    return {
        'x': nrm(ks[0], (BATCH, SEQ, D_MODEL), 1.0),
        'c': nrm(ks[1], (BATCH, D_MODEL), 1.0),
        'w_mod': nrm(ks[2], (L, D_MODEL, N_MOD * D_MODEL), 0.5 * D_MODEL ** -0.5),
        'b_mod': nrm(ks[3], (L, N_MOD * D_MODEL), 0.02),
        'g_mix': 1.0 + nrm(ks[4], (L, D_MODEL), 0.05),
        'w_in': nrm(ks[5], (L, D_MODEL, IN_WIDTH), D_MODEL ** -0.5),
        'g_sgu': 1.0 + nrm(ks[6], (L, SGU_WIDTH), 0.05),
        'w_sgu_s': nrm(ks[7], (L, N_SGU_GROUPS, SGU_CHUNK, SGU_CHUNK), SGU_CHUNK ** -0.5),
        'b_sgu_s': 1.0 + nrm(ks[8], (L, N_SGU_GROUPS, SGU_CHUNK), 0.1),
        'w_attn_br': nrm(ks[9], (L, ATTN_WIDTH, D_MODEL), ATTN_WIDTH ** -0.5),
        'w_sgu_br': nrm(ks[10], (L, SGU_WIDTH, D_MODEL), SGU_WIDTH ** -0.5),
        'w_out': nrm(ks[11], (L, D_MODEL, D_MODEL), D_MODEL ** -0.5),
        'g_ffn': 1.0 + nrm(ks[12], (L, D_MODEL), 0.05),
        'w_up': nrm(ks[13], (L, D_MODEL, 2 * D_FF), D_MODEL ** -0.5),
        'w_conv': nrm(ks[14], (L, CONV_WIDTH, 2 * D_FF), CONV_WIDTH ** -0.5),
        'b_conv': nrm(ks[15], (L, 2 * D_FF), 0.02),
        'w_down': nrm(ks[16], (L, D_FF, D_MODEL), D_FF ** -0.5),
        'g_final': 1.0 + nrm(ks[17], (D_MODEL,), 0.05),
    }


def reference(x, c, w_mod, b_mod, g_mix, w_in, g_sgu, w_sgu_s, b_sgu_s, w_attn_br,
              w_sgu_br, w_out, g_ffn, w_up, w_conv, b_conv, w_down, g_final):
    b, s, _ = x.shape
    a3 = 3 * ATTN_WIDTH
    splits = [ATTN_WIDTH, 2 * ATTN_WIDTH, a3, a3 + SGU_WIDTH, a3 + 2 * SGU_WIDTH,
              a3 + 2 * SGU_WIDTH + D_MODEL]
    c_act = jax.nn.silu(c)
    for l in range(DEPTH):
        mod = c_act @ w_mod[l] + b_mod[l]
        shift_m, scale_m, gate_m, shift_f, scale_f, gate_f = jnp.split(mod, N_MOD, axis=-1)

        h = modulate(rms_norm(x, g_mix[l]), shift_m, scale_m)
        q, k, v, u, vs, gate_a, gate_s = jnp.split(h @ w_in[l], splits, axis=-1)
        head_shape = (b, s, N_ATTN_HEADS, HEAD_DIM)
        y_attn = moba_attention(q.reshape(head_shape), k.reshape(head_shape),
                                v.reshape(head_shape)) @ w_attn_br[l]
        y_sgu = spatial_gating(jax.nn.gelu(u), jax.nn.gelu(vs), g_sgu[l],
                               w_sgu_s[l], b_sgu_s[l]) @ w_sgu_br[l]
        merged = jax.nn.sigmoid(gate_a) * y_attn + jax.nn.sigmoid(gate_s) * y_sgu
        x = x + gate_m[:, None, :] * (merged @ w_out[l])

        h = modulate(rms_norm(x, g_ffn[l]), shift_f, scale_f)
        act, lin = jnp.split(causal_dwconv(h @ w_up[l], w_conv[l], b_conv[l]), 2, axis=-1)
        x = x + gate_f[:, None, :] * ((jax.nn.silu(act) * lin) @ w_down[l])
    return rms_norm(x, g_final)
```

```python
import functools

import jax
import jax.numpy as jnp
from jax import lax
from jax.experimental import pallas as pl
from jax.experimental.pallas import tpu as pltpu

F32 = jnp.float32
BF16 = jnp.bfloat16

N_HEADS = 8
HEAD_DIM = 64
ATTN_WIDTH = N_HEADS * HEAD_DIM
MOBA_BLOCK = 256
MOBA_TOPK = 3
SGU_GROUPS = 8
SGU_GROUP_DIM = 64
SGU_WIDTH = SGU_GROUPS * SGU_GROUP_DIM
SGU_CHUNK = 128
CONV_WIDTH = 3
N_MOD = 6
EPS = 1e-6

LANES = 128
SUBLANES = 8
HEADS_PER_STEP = LANES // HEAD_DIM
VMEM_LIMIT = 56 * 1024 * 1024

TOKEN_TILE = 512
FF_CHUNK = 256
MOD_COLS = 2048


def _dot(a, b):
    return jnp.dot(a, b, preferred_element_type=F32)


def _dot_nt(a, b):
    return lax.dot_general(a, b, (((1,), (1,)), ((), ())), preferred_element_type=F32)


def _sigmoid(x):
    return 1.0 / (1.0 + jnp.exp(-x))


def _gelu_tanh(x):
    c = 0.7978845608028654
    return 0.5 * x * (1.0 + jnp.tanh(c * (x + 0.044715 * (x * x * x))))


def _rms(x, g):
    return x * lax.rsqrt(jnp.mean(x * x, axis=-1, keepdims=True) + EPS) * g


def _norm_mod(x, g, shift, scale):
    return _rms(x, g) * (1.0 + scale) + shift


def _const_spec(shape):
    zeros = (0,) * len(shape)
    return pl.BlockSpec(shape, lambda *_: zeros)


def _mod_kernel(c_ref, w_ref, b_ref, o_ref):
    c = c_ref[...]
    c_act = (c * _sigmoid(c)).astype(BF16)
    o_ref[...] = _dot(c_act, w_ref[...].astype(BF16)) + b_ref[...]


def _modulation(c, w_mod, b_mod):
    depth, d, n = w_mod.shape
    b = c.shape[0]
    return pl.pallas_call(
        _mod_kernel,
        grid=(depth, n // MOD_COLS),
        in_specs=[
            pl.BlockSpec((b, d), lambda l, j: (0, 0)),
            pl.BlockSpec((None, d, MOD_COLS), lambda l, j: (l, 0, j)),
            pl.BlockSpec((None, 1, MOD_COLS), lambda l, j: (l, 0, j)),
        ],
        out_specs=pl.BlockSpec((None, b, MOD_COLS), lambda l, j: (l, 0, j)),
        out_shape=jax.ShapeDtypeStruct((depth, b, n), F32),
        compiler_params=pltpu.CompilerParams(
            dimension_semantics=("arbitrary", "arbitrary"), vmem_limit_bytes=VMEM_LIMIT),
        name="modulation",
    )(c, w_mod, b_mod.reshape(depth, 1, n))


def _mix_in_kernel(x_ref, mod_ref, g_ref, wqkv_ref, wuv_ref, gsgu_ref, wsp_ref, bsp_ref,
                   q_ref, k_ref, v_ref, kmean_ref, sgu_ref):
    tm = x_ref.shape[0]
    h = _norm_mod(x_ref[...], g_ref[...], mod_ref[0:1, :], mod_ref[1:2, :]).astype(BF16)

    qkv = _dot(h, wqkv_ref[...])
    q_ref[...] = (qkv[:, :ATTN_WIDTH] * HEAD_DIM ** -0.5).astype(BF16)
    k = qkv[:, ATTN_WIDTH:2 * ATTN_WIDTH]
    k_ref[...] = k.astype(BF16)
    v_ref[...] = qkv[:, 2 * ATTN_WIDTH:].astype(BF16)
    for c in range(tm // MOBA_BLOCK):
        kmean_ref[c] = jnp.mean(k[c * MOBA_BLOCK:(c + 1) * MOBA_BLOCK], axis=0, keepdims=True)

    uv = _dot(h, wuv_ref[...])
    gu = _gelu_tanh(uv[:, :SGU_WIDTH])
    vn = _rms(_gelu_tanh(uv[:, SGU_WIDTH:]), gsgu_ref[...]).astype(BF16)

    lane = lax.broadcasted_iota(jnp.int32, (SGU_CHUNK, LANES), 1)
    first = lane < SGU_GROUP_DIM
    wrow = lax.broadcasted_iota(jnp.int32, (SGU_CHUNK, 2 * SGU_CHUNK), 0)
    wcol = lax.broadcasted_iota(jnp.int32, (SGU_CHUNK, 2 * SGU_CHUNK), 1)
    causal = jnp.where(wcol >= SGU_CHUNK, wcol - SGU_CHUNK, wcol) <= wrow
    zero = jnp.zeros((), BF16)
    for gp in range(SGU_WIDTH // LANES):
        cols = slice(gp * LANES, (gp + 1) * LANES)
        w = jnp.where(causal, wsp_ref[gp], zero)
        bias = bsp_ref[:, cols]
        for c in range(tm // SGU_CHUNK):
            rows = slice(c * SGU_CHUNK, (c + 1) * SGU_CHUNK)
            vb = vn[rows, cols]
            rhs = jnp.concatenate([jnp.where(first, vb, zero), jnp.where(first, zero, vb)], axis=0)
            mixed = _dot(w, rhs) + bias
            sgu_ref[rows, cols] = (gu[rows, cols] * mixed).astype(BF16)


def _mix_in(x, mod, g_mix, w_qkv, w_uv, g_sgu, w_sp, b_sp):
    b, s, d = x.shape
    tm = TOKEN_TILE
    blocks_per_tile = tm // MOBA_BLOCK
    act = lambda width: jax.ShapeDtypeStruct((b, s, width), BF16)
    act_spec = lambda width: pl.BlockSpec((None, tm, width), lambda bi, i: (bi, i, 0))
    return pl.pallas_call(
        _mix_in_kernel,
        grid=(b, s // tm),
        in_specs=[
            pl.BlockSpec((None, tm, d), lambda bi, i: (bi, i, 0)),
            pl.BlockSpec((None, N_MOD, d), lambda bi, i: (bi, 0, 0)),
            _const_spec(g_mix.shape),
            _const_spec(w_qkv.shape),
            _const_spec(w_uv.shape),
            _const_spec(g_sgu.shape),
            _const_spec(w_sp.shape),
            _const_spec(b_sp.shape),
        ],
        out_specs=[
            act_spec(ATTN_WIDTH), act_spec(ATTN_WIDTH), act_spec(ATTN_WIDTH),
            pl.BlockSpec((None, blocks_per_tile, 1, ATTN_WIDTH), lambda bi, i: (bi, i, 0, 0)),
            act_spec(SGU_WIDTH),
        ],
        out_shape=[
            act(ATTN_WIDTH), act(ATTN_WIDTH), act(ATTN_WIDTH),
            jax.ShapeDtypeStruct((b, s // MOBA_BLOCK, 1, ATTN_WIDTH), F32),
            act(SGU_WIDTH),
        ],
        compiler_params=pltpu.CompilerParams(
            dimension_semantics=("arbitrary", "arbitrary"), vmem_limit_bytes=VMEM_LIMIT),
        name="mix_in",
    )(x, mod, g_mix, w_qkv, w_uv, g_sgu, w_sp, b_sp)


def _moba_kernel(q_ref, k_ref, v_ref, km_ref, o_ref, sel_ref, m_ref, l_ref, acc_ref):
    i = pl.program_id(2)
    nb = km_ref.shape[0]
    tq = q_ref.shape[0]
    q = q_ref[...]
    lane = lax.broadcasted_iota(jnp.int32, (tq, LANES), 1)
    neg_inf = jnp.float32(-jnp.inf)

    km = km_ref[...]
    km_hi = km.astype(BF16)
    km_lo = (km - km_hi.astype(F32)).astype(BF16)
    blk = lax.broadcasted_iota(jnp.int32, (tq, nb), 1)
    past = blk < i

    own = pl.multiple_of(i * MOBA_BLOCK, MOBA_BLOCK)
    k_own = k_ref[pl.ds(own, MOBA_BLOCK), :]
    v_own = v_ref[pl.ds(own, MOBA_BLOCK), :]
    row = lax.broadcasted_iota(jnp.int32, (tq, MOBA_BLOCK), 0)
    col = lax.broadcasted_iota(jnp.int32, (tq, MOBA_BLOCK), 1)

    qh = []
    for h in range(HEADS_PER_STEP):
        in_head = (lane >= h * HEAD_DIM) & (lane < (h + 1) * HEAD_DIM)
        qh.append(jnp.where(in_head, q, jnp.zeros((), BF16)))

        route = _dot_nt(qh[h], km_hi) + _dot_nt(qh[h], km_lo)
        rank = jnp.zeros((tq, nb), jnp.int32)
        for jp in range(nb):
            rj = route[:, jp:jp + 1]
            beats = (rj > route) | ((rj == route) & (blk > jp))
            rank = rank + jnp.where(beats & (i > jp), 1, 0)
        keep = jnp.where(past & (rank < MOBA_TOPK), 1.0, 0.0).astype(F32)
        for j in range(nb):
            sel_ref[h, j] = jnp.broadcast_to(keep[:, j:j + 1], (tq, LANES))

        s = jnp.where(col <= row, _dot_nt(qh[h], k_own), neg_inf)
        m = jnp.max(s, axis=1, keepdims=True)
        p = jnp.exp(s - m)
        m_ref[h] = m
        l_ref[h] = jnp.sum(p, axis=1, keepdims=True)
        acc_ref[h] = _dot(p.astype(BF16), v_own)

    def past_block(j, carry):
        start = pl.multiple_of(j * MOBA_BLOCK, MOBA_BLOCK)
        kj = k_ref[pl.ds(start, MOBA_BLOCK), :]
        vj = v_ref[pl.ds(start, MOBA_BLOCK), :]
        for h in range(HEADS_PER_STEP):
            keep = sel_ref[h, j] > 0.0
            keep = jnp.concatenate([keep] * (MOBA_BLOCK // LANES), axis=1)
            s = jnp.where(keep, _dot_nt(qh[h], kj), neg_inf)
            m_prev = m_ref[h]
            m_new = jnp.maximum(m_prev, jnp.max(s, axis=1, keepdims=True))
            alpha = jnp.exp(m_prev - m_new)
            p = jnp.exp(s - m_new)
            m_ref[h] = m_new
            l_ref[h] = alpha * l_ref[h] + jnp.sum(p, axis=1, keepdims=True)
            acc_ref[h] = alpha * acc_ref[h] + _dot(p.astype(BF16), vj)
        return carry

    lax.fori_loop(0, i, past_block, 0)

    out = acc_ref[0] / l_ref[0]
    for h in range(1, HEADS_PER_STEP):
        out = jnp.where(lane >= h * HEAD_DIM, acc_ref[h] / l_ref[h], out)
    o_ref[...] = out.astype(o_ref.dtype)


def _moba(q, k, v, kmean):
    b, s, width = q.shape
    nb = s // MOBA_BLOCK
    tq = MOBA_BLOCK
    return pl.pallas_call(
        _moba_kernel,
        grid=(b, width // LANES, nb),
        in_specs=[
            pl.BlockSpec((None, tq, LANES), lambda bi, p, i: (bi, i, p)),
            pl.BlockSpec((None, s, LANES), lambda bi, p, i: (bi, 0, p)),
            pl.BlockSpec((None, s, LANES), lambda bi, p, i: (bi, 0, p)),
            pl.BlockSpec((None, nb, LANES), lambda bi, p, i: (bi, 0, p)),
        ],
        out_specs=pl.BlockSpec((None, tq, LANES), lambda bi, p, i: (bi, i, p)),
        out_shape=jax.ShapeDtypeStruct((b, s, width), BF16),
        scratch_shapes=[
            pltpu.VMEM((HEADS_PER_STEP, nb, tq, LANES), F32),
            pltpu.VMEM((HEADS_PER_STEP, tq, 1), F32),
            pltpu.VMEM((HEADS_PER_STEP, tq, 1), F32),
            pltpu.VMEM((HEADS_PER_STEP, tq, LANES), F32),
        ],
        compiler_params=pltpu.CompilerParams(
            dimension_semantics=("arbitrary", "arbitrary", "arbitrary"),
            vmem_limit_bytes=VMEM_LIMIT),
        name="moba",
    )(q, k, v, kmean)


def _mix_out_kernel(x_ref, mod_ref, g_ref, attn_ref, sgu_ref, wgate_ref, wa_ref, ws_ref, wo_ref,
                    o_ref):
    d = x_ref.shape[1]
    x = x_ref[...]
    h = _norm_mod(x, g_ref[...], mod_ref[0:1, :], mod_ref[1:2, :]).astype(BF16)
    gates = _dot(h, wgate_ref[...])
    ya = _dot(attn_ref[...], wa_ref[...])
    ys = _dot(sgu_ref[...], ws_ref[...])
    merged = _sigmoid(gates[:, :d]) * ya + _sigmoid(gates[:, d:]) * ys
    o_ref[...] = x + mod_ref[2:3, :] * _dot(merged.astype(BF16), wo_ref[...])


def _mix_out(x, mod, g_mix, attn, sgu, w_gate, w_attn_br, w_sgu_br, w_out):
    b, s, d = x.shape
    tm = TOKEN_TILE
    tile = lambda width: pl.BlockSpec((None, tm, width), lambda bi, i: (bi, i, 0))
    return pl.pallas_call(
        _mix_out_kernel,
        grid=(b, s // tm),
        in_specs=[
            tile(d),
            pl.BlockSpec((None, N_MOD, d), lambda bi, i: (bi, 0, 0)),
            _const_spec(g_mix.shape),
            tile(attn.shape[2]),
            tile(sgu.shape[2]),
            _const_spec(w_gate.shape),
            _const_spec(w_attn_br.shape),
            _const_spec(w_sgu_br.shape),
            _const_spec(w_out.shape),
        ],
        out_specs=tile(d),
        out_shape=jax.ShapeDtypeStruct(x.shape, F32),
        compiler_params=pltpu.CompilerParams(
            dimension_semantics=("arbitrary", "arbitrary"), vmem_limit_bytes=VMEM_LIMIT),
        name="mix_out",
    )(x, mod, g_mix, attn, sgu, w_gate, w_attn_br, w_sgu_br, w_out)


def _conv_ffn_kernel(x_ref, mod_ref, g_ref, wup_ref, conv_ref, wdown_ref, gfin_ref, o_ref,
                     h_ref, acc_ref, tail_ref, *, final_norm):
    i = pl.program_id(1)
    tm = x_ref.shape[0]
    n_chunks, _, two_tf = wup_ref.shape
    tf = two_tf // 2
    x = x_ref[...]
    h_ref[...] = _norm_mod(x, g_ref[...], mod_ref[3:4, :], mod_ref[4:5, :]).astype(BF16)
    acc_ref[...] = jnp.zeros_like(acc_ref)
    head_row = lax.broadcasted_iota(jnp.int32, (SUBLANES, two_tf), 0)

    def chunk(j, carry):
        up = _dot(h_ref[...], wup_ref[j])
        prev = jnp.where(i > 0, tail_ref[j], 0.0)
        tail_ref[j] = up[tm - SUBLANES:, :]
        taps = conv_ref[j]
        conv = taps[3:4, :] + taps[2:3, :] * up
        for back in (1, 2):
            shifted = pltpu.roll(up, back, 0)
            head = jnp.where(head_row < back, pltpu.roll(prev, back, 0), shifted[:SUBLANES])
            shifted = jnp.concatenate([head, shifted[SUBLANES:]], axis=0)
            conv = conv + taps[2 - back:3 - back, :] * shifted
        act = conv[:, :tf]
        gated = (act * _sigmoid(act) * conv[:, tf:]).astype(BF16)
        acc_ref[...] += _dot(gated, wdown_ref[j])
        return carry

    lax.fori_loop(0, n_chunks, chunk, 0)
    out = x + mod_ref[5:6, :] * acc_ref[...]
    if final_norm:
        out = _rms(out, gfin_ref[...])
    o_ref[...] = out


def _conv_ffn(x, mod, g_ffn, w_up_c, conv_c, w_down_c, g_final, final_norm):
    b, s, d = x.shape
    tm = TOKEN_TILE
    n_chunks, _, two_tf = w_up_c.shape
    tile = pl.BlockSpec((None, tm, d), lambda bi, i: (bi, i, 0))
    return pl.pallas_call(
        functools.partial(_conv_ffn_kernel, final_norm=final_norm),
        grid=(b, s // tm),
        in_specs=[
            tile,
            pl.BlockSpec((None, N_MOD, d), lambda bi, i: (bi, 0, 0)),
            _const_spec(g_ffn.shape),
            _const_spec(w_up_c.shape),
            _const_spec(conv_c.shape),
            _const_spec(w_down_c.shape),
            _const_spec(g_final.shape),
        ],
        out_specs=tile,
        out_shape=jax.ShapeDtypeStruct(x.shape, F32),
        scratch_shapes=[
            pltpu.VMEM((tm, d), BF16),
            pltpu.VMEM((tm, d), F32),
            pltpu.VMEM((n_chunks, SUBLANES, two_tf), F32),
        ],
        compiler_params=pltpu.CompilerParams(
            dimension_semantics=("arbitrary", "arbitrary"), vmem_limit_bytes=VMEM_LIMIT),
        name="conv_ffn",
    )(x, mod, g_ffn, w_up_c, conv_c, w_down_c, g_final)


def _prep_layer(w_in, w_sgu_s, b_sgu_s, w_attn_br, w_sgu_br, w_out, w_up, w_conv, b_conv, w_down):
    d = w_in.shape[0]
    a3 = 3 * ATTN_WIDTH
    w_qkv = w_in[:, :a3].astype(BF16)
    w_uv = w_in[:, a3:a3 + 2 * SGU_WIDTH].astype(BF16)
    w_gate = w_in[:, a3 + 2 * SGU_WIDTH:].astype(BF16)
    w_sp = (w_sgu_s.reshape(SGU_GROUPS // 2, 2, SGU_CHUNK, SGU_CHUNK)
            .transpose(0, 2, 1, 3).reshape(SGU_GROUPS // 2, SGU_CHUNK, 2 * SGU_CHUNK).astype(BF16))
    b_sp = jnp.repeat(b_sgu_s.T, SGU_GROUP_DIM, axis=1)
    d_ff = w_down.shape[0]
    n_chunks = d_ff // FF_CHUNK

    def chunked(a):
        r = a.shape[0]
        a = a.reshape(r, 2, n_chunks, FF_CHUNK).transpose(2, 0, 1, 3)
        return a.reshape(n_chunks, r, 2 * FF_CHUNK)

    w_up_c = chunked(w_up).astype(BF16)
    taps = jnp.concatenate(
        [w_conv, b_conv[None, :], jnp.zeros((SUBLANES - CONV_WIDTH - 1, 2 * d_ff), F32)], axis=0)
    conv_c = chunked(taps)
    w_down_c = w_down.reshape(n_chunks, FF_CHUNK, d).astype(BF16)
    return (w_qkv, w_uv, w_gate, w_sp, b_sp, w_attn_br.astype(BF16), w_sgu_br.astype(BF16),
            w_out.astype(BF16), w_up_c, conv_c, w_down_c)


def kernel(x, c, w_mod, b_mod, g_mix, w_in, g_sgu, w_sgu_s, b_sgu_s, w_attn_br, w_sgu_br, w_out,
           g_ffn, w_up, w_conv, b_conv, w_down, g_final):
    b, s, d = x.shape
    depth = w_mod.shape[0]
    assert s % TOKEN_TILE == 0 and TOKEN_TILE % MOBA_BLOCK == 0 and w_down.shape[1] % FF_CHUNK == 0
    mod_all = _modulation(c, w_mod, b_mod).reshape(depth, b, N_MOD, d)
    g_fin = g_final.reshape(1, d)
    for l in range(depth):
        (w_qkv, w_uv, w_gate, w_sp, b_sp, w_a, w_s, w_o, w_up_c, conv_c, w_down_c) = _prep_layer(
            w_in[l], w_sgu_s[l], b_sgu_s[l], w_attn_br[l], w_sgu_br[l], w_out[l], w_up[l],
            w_conv[l], b_conv[l], w_down[l])
        mod = mod_all[l]
        g_m = g_mix[l].reshape(1, d)
        q, k, v, kmean, sgu = _mix_in(x, mod, g_m, w_qkv, w_uv, g_sgu[l].reshape(1, SGU_WIDTH),
                                      w_sp, b_sp)
        attn = _moba(q, k, v, kmean.reshape(b, s // MOBA_BLOCK, ATTN_WIDTH))
        x = _mix_out(x, mod, g_m, attn, sgu, w_gate, w_a, w_s, w_o)
        x = _conv_ffn(x, mod, g_ffn[l].reshape(1, d), w_up_c, conv_c, w_down_c, g_fin,
                      final_norm=(l == depth - 1))
    return x
```

```python
import functools

import jax
import jax.numpy as jnp
from jax import lax
from jax.experimental import pallas as pl
from jax.experimental.pallas import tpu as pltpu

F32 = jnp.float32
BF16 = jnp.bfloat16

N_HEADS = 8
HEAD_DIM = 64
ATTN_WIDTH = N_HEADS * HEAD_DIM
MOBA_BLOCK = 256
MOBA_TOPK = 3
SGU_GROUPS = 8
SGU_GROUP_DIM = 64
SGU_WIDTH = SGU_GROUPS * SGU_GROUP_DIM
SGU_CHUNK = 128
CONV_WIDTH = 3
N_MOD = 6
EPS = 1e-6

LANES = 128
SUBLANES = 8
HEADS_PER_STEP = LANES // HEAD_DIM
VMEM_LIMIT = 56 * 1024 * 1024

TOKEN_TILE = 512
FF_CHUNK = 256
MOD_COLS = 2048


def _dot(a, b):
    return jnp.dot(a, b, preferred_element_type=F32)


def _dot_nt(a, b):
    return lax.dot_general(a, b, (((1,), (1,)), ((), ())), preferred_element_type=F32)


def _sigmoid(x):
    return 1.0 / (1.0 + jnp.exp(-x))


def _gelu_tanh(x):
    c = 0.7978845608028654
    return 0.5 * x * (1.0 + jnp.tanh(c * (x + 0.044715 * (x * x * x))))


def _rms(x, g):
    return x * lax.rsqrt(jnp.mean(x * x, axis=-1, keepdims=True) + EPS) * g


def _norm_mod(x, g, shift, scale):
    return _rms(x, g) * (1.0 + scale) + shift


def _const_spec(shape):
    zeros = (0,) * len(shape)
    return pl.BlockSpec(shape, lambda *_: zeros)


def _mod_kernel(c_ref, w_ref, b_ref, o_ref):
    c = c_ref[...]
    c_act = (c * _sigmoid(c)).astype(BF16)
    o_ref[...] = _dot(c_act, w_ref[...].astype(BF16)) + b_ref[...]


def _modulation(c, w_mod, b_mod):
    depth, d, n = w_mod.shape
    b = c.shape[0]
    return pl.pallas_call(
        _mod_kernel,
        grid=(depth, n // MOD_COLS),
        in_specs=[
            pl.BlockSpec((b, d), lambda l, j: (0, 0)),
            pl.BlockSpec((None, d, MOD_COLS), lambda l, j: (l, 0, j)),
            pl.BlockSpec((None, 1, MOD_COLS), lambda l, j: (l, 0, j)),
        ],
        out_specs=pl.BlockSpec((None, b, MOD_COLS), lambda l, j: (l, 0, j)),
        out_shape=jax.ShapeDtypeStruct((depth, b, n), F32),
        compiler_params=pltpu.CompilerParams(
            dimension_semantics=("arbitrary", "arbitrary"), vmem_limit_bytes=VMEM_LIMIT),
        name="modulation",
    )(c, w_mod, b_mod.reshape(depth, 1, n))


def _mix_in_kernel(x_ref, mod_ref, g_ref, wqkv_ref, wuv_ref, gsgu_ref, wsp_ref, bsp_ref,
                   q_ref, k_ref, v_ref, kmean_ref, sgu_ref):
    tm = x_ref.shape[0]
    h = _norm_mod(x_ref[...], g_ref[...], mod_ref[0:1, :], mod_ref[1:2, :]).astype(BF16)

    qkv = _dot(h, wqkv_ref[...])
    q_ref[...] = (qkv[:, :ATTN_WIDTH] * HEAD_DIM ** -0.5).astype(BF16)
    k = qkv[:, ATTN_WIDTH:2 * ATTN_WIDTH]
    k_ref[...] = k.astype(BF16)
    v_ref[...] = qkv[:, 2 * ATTN_WIDTH:].astype(BF16)
    for c in range(tm // MOBA_BLOCK):
        kmean_ref[c] = jnp.mean(k[c * MOBA_BLOCK:(c + 1) * MOBA_BLOCK], axis=0, keepdims=True)

    uv = _dot(h, wuv_ref[...])
    gu = _gelu_tanh(uv[:, :SGU_WIDTH])
    vn = _rms(_gelu_tanh(uv[:, SGU_WIDTH:]), gsgu_ref[...]).astype(BF16)

    lane = lax.broadcasted_iota(jnp.int32, (SGU_CHUNK, LANES), 1)
    first = lane < SGU_GROUP_DIM
    wrow = lax.broadcasted_iota(jnp.int32, (SGU_CHUNK, 2 * SGU_CHUNK), 0)
    wcol = lax.broadcasted_iota(jnp.int32, (SGU_CHUNK, 2 * SGU_CHUNK), 1)
    causal = jnp.where(wcol >= SGU_CHUNK, wcol - SGU_CHUNK, wcol) <= wrow
    zero = jnp.zeros((), BF16)
    for gp in range(SGU_WIDTH // LANES):
        cols = slice(gp * LANES, (gp + 1) * LANES)
        w = jnp.where(causal, wsp_ref[gp], zero)
        bias = bsp_ref[:, cols]
        for c in range(tm // SGU_CHUNK):
            rows = slice(c * SGU_CHUNK, (c + 1) * SGU_CHUNK)
            vb = vn[rows, cols]
            rhs = jnp.concatenate([jnp.where(first, vb, zero), jnp.where(first, zero, vb)], axis=0)
            mixed = _dot(w, rhs) + bias
            sgu_ref[rows, cols] = (gu[rows, cols] * mixed).astype(BF16)


def _mix_in(x, mod, g_mix, w_qkv, w_uv, g_sgu, w_sp, b_sp):
    b, s, d = x.shape
    tm = TOKEN_TILE
    blocks_per_tile = tm // MOBA_BLOCK
    act = lambda width: jax.ShapeDtypeStruct((b, s, width), BF16)
    act_spec = lambda width: pl.BlockSpec((None, tm, width), lambda bi, i: (bi, i, 0))
    return pl.pallas_call(
        _mix_in_kernel,
        grid=(b, s // tm),
        in_specs=[
            pl.BlockSpec((None, tm, d), lambda bi, i: (bi, i, 0)),
            pl.BlockSpec((None, N_MOD, d), lambda bi, i: (bi, 0, 0)),
            _const_spec(g_mix.shape),
            _const_spec(w_qkv.shape),
            _const_spec(w_uv.shape),
            _const_spec(g_sgu.shape),
            _const_spec(w_sp.shape),
            _const_spec(b_sp.shape),
        ],
        out_specs=[
            act_spec(ATTN_WIDTH), act_spec(ATTN_WIDTH), act_spec(ATTN_WIDTH),
            pl.BlockSpec((None, blocks_per_tile, 1, ATTN_WIDTH), lambda bi, i: (bi, i, 0, 0)),
            act_spec(SGU_WIDTH),
        ],
        out_shape=[
            act(ATTN_WIDTH), act(ATTN_WIDTH), act(ATTN_WIDTH),
            jax.ShapeDtypeStruct((b, s // MOBA_BLOCK, 1, ATTN_WIDTH), F32),
            act(SGU_WIDTH),
        ],
        compiler_params=pltpu.CompilerParams(
            dimension_semantics=("arbitrary", "arbitrary"), vmem_limit_bytes=VMEM_LIMIT),
        name="mix_in",
    )(x, mod, g_mix, w_qkv, w_uv, g_sgu, w_sp, b_sp)


def _moba_kernel(q_ref, k_ref, v_ref, km_ref, o_ref, vt_ref, ot_ref):
    nb = km_ref.shape[0]
    s_len = q_ref.shape[0]
    neg_inf = jnp.float32(-jnp.inf)
    blocks = [slice(j * MOBA_BLOCK, (j + 1) * MOBA_BLOCK) for j in range(nb)]

    for j in range(nb):
        vt_ref[:, blocks[j]] = v_ref[blocks[j], :].astype(F32).T.astype(BF16)

    q = q_ref[...]
    lane = lax.broadcasted_iota(jnp.int32, (s_len, LANES), 1)
    km = km_ref[...]
    km_hi = km.astype(BF16)
    km_lo = (km - km_hi.astype(F32)).astype(BF16)
    blk = lax.broadcasted_iota(jnp.int32, (nb, s_len), 0)
    q_blk = lax.broadcasted_iota(jnp.int32, (nb, s_len), 1) // MOBA_BLOCK
    kpos = lax.broadcasted_iota(jnp.int32, (MOBA_BLOCK, MOBA_BLOCK), 0)
    qpos = lax.broadcasted_iota(jnp.int32, (MOBA_BLOCK, MOBA_BLOCK), 1)
    causal = kpos <= qpos

    for h in range(HEADS_PER_STEP):
        rows = slice(h * HEAD_DIM, (h + 1) * HEAD_DIM)
        in_head = (lane >= h * HEAD_DIM) & (lane < (h + 1) * HEAD_DIM)
        qh = jnp.where(in_head, q, jnp.zeros((), BF16))

        route = _dot_nt(km_hi, qh) + _dot_nt(km_lo, qh)
        rank = jnp.zeros((nb, s_len), jnp.int32)
        for jp in range(nb):
            rj = route[jp:jp + 1, :]
            beats = (rj > route) | ((rj == route) & (blk > jp))
            rank = rank + jnp.where(beats & (q_blk > jp), 1, 0)
        bias = jnp.where((blk < q_blk) & (rank < MOBA_TOPK), 0.0, neg_inf)

        for i in range(nb):
            n_keys = (i + 1) * MOBA_BLOCK
            s = _dot_nt(k_ref[:n_keys, :], qh[blocks[i]])
            parts = [s[blocks[j]] + bias[j:j + 1, blocks[i]] for j in range(i)]
            parts.append(jnp.where(causal, s[blocks[i]], neg_inf))
            s = jnp.concatenate(parts, axis=0)
            m = jnp.max(s, axis=0, keepdims=True)
            p = jnp.exp(s - m)
            l = jnp.sum(p, axis=0, keepdims=True)
            ot_ref[rows, blocks[i]] = _dot(vt_ref[rows, :n_keys], p.astype(BF16)) / l

    for i in range(nb):
        o_ref[blocks[i], :] = ot_ref[:, blocks[i]].T.astype(o_ref.dtype)


def _moba(q, k, v, kmean):
    b, s, width = q.shape
    nb = s // MOBA_BLOCK
    slab = pl.BlockSpec((None, s, LANES), lambda bi, p: (bi, 0, p))
    return pl.pallas_call(
        _moba_kernel,
        grid=(b, width // LANES),
        in_specs=[slab, slab, slab, pl.BlockSpec((None, nb, LANES), lambda bi, p: (bi, 0, p))],
        out_specs=slab,
        out_shape=jax.ShapeDtypeStruct((b, s, width), BF16),
        scratch_shapes=[pltpu.VMEM((LANES, s), BF16), pltpu.VMEM((LANES, s), F32)],
        compiler_params=pltpu.CompilerParams(
            dimension_semantics=("arbitrary", "arbitrary"), vmem_limit_bytes=VMEM_LIMIT),
        name="moba",
    )(q, k, v, kmean)


def _mix_out_kernel(x_ref, mod_ref, g_ref, attn_ref, sgu_ref, wgate_ref, wa_ref, ws_ref, wo_ref,
                    o_ref):
    d = x_ref.shape[1]
    x = x_ref[...]
    h = _norm_mod(x, g_ref[...], mod_ref[0:1, :], mod_ref[1:2, :]).astype(BF16)
    gates = _dot(h, wgate_ref[...])
    ya = _dot(attn_ref[...], wa_ref[...])
    ys = _dot(sgu_ref[...], ws_ref[...])
    merged = _sigmoid(gates[:, :d]) * ya + _sigmoid(gates[:, d:]) * ys
    o_ref[...] = x + mod_ref[2:3, :] * _dot(merged.astype(BF16), wo_ref[...])


def _mix_out(x, mod, g_mix, attn, sgu, w_gate, w_attn_br, w_sgu_br, w_out):
    b, s, d = x.shape
    tm = TOKEN_TILE
    tile = lambda width: pl.BlockSpec((None, tm, width), lambda bi, i: (bi, i, 0))
    return pl.pallas_call(
        _mix_out_kernel,
        grid=(b, s // tm),
        in_specs=[
            tile(d),
            pl.BlockSpec((None, N_MOD, d), lambda bi, i: (bi, 0, 0)),
            _const_spec(g_mix.shape),
            tile(attn.shape[2]),
            tile(sgu.shape[2]),
            _const_spec(w_gate.shape),
            _const_spec(w_attn_br.shape),
            _const_spec(w_sgu_br.shape),
            _const_spec(w_out.shape),
        ],
        out_specs=tile(d),
        out_shape=jax.ShapeDtypeStruct(x.shape, F32),
        compiler_params=pltpu.CompilerParams(
            dimension_semantics=("arbitrary", "arbitrary"), vmem_limit_bytes=VMEM_LIMIT),
        name="mix_out",
    )(x, mod, g_mix, attn, sgu, w_gate, w_attn_br, w_sgu_br, w_out)


def _conv_ffn_kernel(x_ref, mod_ref, g_ref, wup_ref, conv_ref, wdown_ref, gfin_ref, o_ref,
                     h_ref, acc_ref, tail_ref, *, final_norm):
    i = pl.program_id(1)
    tm = x_ref.shape[0]
    n_chunks, _, two_tf = wup_ref.shape
    tf = two_tf // 2
    x = x_ref[...]
    h_ref[...] = _norm_mod(x, g_ref[...], mod_ref[3:4, :], mod_ref[4:5, :]).astype(BF16)
    acc_ref[...] = jnp.zeros_like(acc_ref)
    head_row = lax.broadcasted_iota(jnp.int32, (SUBLANES, two_tf), 0)

    def chunk(j, carry):
        up = _dot(h_ref[...], wup_ref[j])
        prev = jnp.where(i > 0, tail_ref[j], 0.0)
        tail_ref[j] = up[tm - SUBLANES:, :]
        taps = conv_ref[j]
        conv = taps[3:4, :] + taps[2:3, :] * up
        for back in (1, 2):
            shifted = pltpu.roll(up, back, 0)
            head = jnp.where(head_row < back, pltpu.roll(prev, back, 0), shifted[:SUBLANES])
            shifted = jnp.concatenate([head, shifted[SUBLANES:]], axis=0)
            conv = conv + taps[2 - back:3 - back, :] * shifted
        act = conv[:, :tf]
        gated = (act * _sigmoid(act) * conv[:, tf:]).astype(BF16)
        acc_ref[...] += _dot(gated, wdown_ref[j])
        return carry

    lax.fori_loop(0, n_chunks, chunk, 0)
    out = x + mod_ref[5:6, :] * acc_ref[...]
    if final_norm:
        out = _rms(out, gfin_ref[...])
    o_ref[...] = out


def _conv_ffn(x, mod, g_ffn, w_up_c, conv_c, w_down_c, g_final, final_norm):
    b, s, d = x.shape
    tm = TOKEN_TILE
    n_chunks, _, two_tf = w_up_c.shape
    tile = pl.BlockSpec((None, tm, d), lambda bi, i: (bi, i, 0))
    return pl.pallas_call(
        functools.partial(_conv_ffn_kernel, final_norm=final_norm),
        grid=(b, s // tm),
        in_specs=[
            tile,
            pl.BlockSpec((None, N_MOD, d), lambda bi, i: (bi, 0, 0)),
            _const_spec(g_ffn.shape),
            _const_spec(w_up_c.shape),
            _const_spec(conv_c.shape),
            _const_spec(w_down_c.shape),
            _const_spec(g_final.shape),
        ],
        out_specs=tile,
        out_shape=jax.ShapeDtypeStruct(x.shape, F32),
        scratch_shapes=[
            pltpu.VMEM((tm, d), BF16),
            pltpu.VMEM((tm, d), F32),
            pltpu.VMEM((n_chunks, SUBLANES, two_tf), F32),
        ],
        compiler_params=pltpu.CompilerParams(
            dimension_semantics=("arbitrary", "arbitrary"), vmem_limit_bytes=VMEM_LIMIT),
        name="conv_ffn",
    )(x, mod, g_ffn, w_up_c, conv_c, w_down_c, g_final)


def _prep_layer(w_in, w_sgu_s, b_sgu_s, w_attn_br, w_sgu_br, w_out, w_up, w_conv, b_conv, w_down):
    d = w_in.shape[0]
    a3 = 3 * ATTN_WIDTH
    w_qkv = w_in[:, :a3].astype(BF16)
    w_uv = w_in[:, a3:a3 + 2 * SGU_WIDTH].astype(BF16)
    w_gate = w_in[:, a3 + 2 * SGU_WIDTH:].astype(BF16)
    w_sp = (w_sgu_s.reshape(SGU_GROUPS // 2, 2, SGU_CHUNK, SGU_CHUNK)
            .transpose(0, 2, 1, 3).reshape(SGU_GROUPS // 2, SGU_CHUNK, 2 * SGU_CHUNK).astype(BF16))
    b_sp = jnp.repeat(b_sgu_s.T, SGU_GROUP_DIM, axis=1)
    d_ff = w_down.shape[0]
    n_chunks = d_ff // FF_CHUNK

    def chunked(a):
        r = a.shape[0]
        a = a.reshape(r, 2, n_chunks, FF_CHUNK).transpose(2, 0, 1, 3)
        return a.reshape(n_chunks, r, 2 * FF_CHUNK)

    w_up_c = chunked(w_up).astype(BF16)
    taps = jnp.concatenate(
        [w_conv, b_conv[None, :], jnp.zeros((SUBLANES - CONV_WIDTH - 1, 2 * d_ff), F32)], axis=0)
    conv_c = chunked(taps)
    w_down_c = w_down.reshape(n_chunks, FF_CHUNK, d).astype(BF16)
    return (w_qkv, w_uv, w_gate, w_sp, b_sp, w_attn_br.astype(BF16), w_sgu_br.astype(BF16),
            w_out.astype(BF16), w_up_c, conv_c, w_down_c)


def kernel(x, c, w_mod, b_mod, g_mix, w_in, g_sgu, w_sgu_s, b_sgu_s, w_attn_br, w_sgu_br, w_out,
           g_ffn, w_up, w_conv, b_conv, w_down, g_final):
    b, s, d = x.shape
    depth = w_mod.shape[0]
    assert s % TOKEN_TILE == 0 and TOKEN_TILE % MOBA_BLOCK == 0 and w_down.shape[1] % FF_CHUNK == 0
    mod_all = _modulation(c, w_mod, b_mod).reshape(depth, b, N_MOD, d)
    g_fin = g_final.reshape(1, d)
    for l in range(depth):
        (w_qkv, w_uv, w_gate, w_sp, b_sp, w_a, w_s, w_o, w_up_c, conv_c, w_down_c) = _prep_layer(
            w_in[l], w_sgu_s[l], b_sgu_s[l], w_attn_br[l], w_sgu_br[l], w_out[l], w_up[l],
            w_conv[l], b_conv[l], w_down[l])
        mod = mod_all[l]
        g_m = g_mix[l].reshape(1, d)
        q, k, v, kmean, sgu = _mix_in(x, mod, g_m, w_qkv, w_uv, g_sgu[l].reshape(1, SGU_WIDTH),
                                      w_sp, b_sp)
        attn = _moba(q, k, v, kmean.reshape(b, s // MOBA_BLOCK, ATTN_WIDTH))
        x = _mix_out(x, mod, g_m, attn, sgu, w_gate, w_a, w_s, w_o)
        x = _conv_ffn(x, mod, g_ffn[l].reshape(1, d), w_up_c, conv_c, w_down_c, g_fin,
                      final_norm=(l == depth - 1))
    return x
```

```python
import functools
import math

import jax
import jax.numpy as jnp
from jax import lax
from jax.experimental import pallas as pl
from jax.experimental.pallas import tpu as pltpu

F32 = jnp.float32
BF16 = jnp.bfloat16

N_HEADS = 8
HEAD_DIM = 64
ATTN_WIDTH = N_HEADS * HEAD_DIM
MOBA_BLOCK = 256
MOBA_TOPK = 3
SGU_GROUPS = 8
SGU_GROUP_DIM = 64
SGU_WIDTH = SGU_GROUPS * SGU_GROUP_DIM
SGU_CHUNK = 128
CONV_WIDTH = 3
N_MOD = 6
EPS = 1e-6
QKV_WIDTH = 3 * ATTN_WIDTH
UV_WIDTH = 2 * SGU_WIDTH

LANES = 128
SUBLANES = 8
BF16_ROWS = 16
HEADS_PER_STEP = LANES // HEAD_DIM
VMEM_LIMIT = 56 * 1024 * 1024

TOKEN_TILE = 512
FF_CHUNK = 256
MOD_COLS = 2048
SCORE_LOOKAHEAD = 3

Q_SCALE = HEAD_DIM ** -0.5 * math.log2(math.e)


def _dot(a, b):
    return jnp.dot(a, b, preferred_element_type=F32)


def _dot_nt(a, b):
    return lax.dot_general(a, b, (((1,), (1,)), ((), ())), preferred_element_type=F32)


def _sigmoid(x):
    return 1.0 / (1.0 + jnp.exp(-x))


def _gelu_tanh(x):
    c = 0.7978845608028654
    return 0.5 * x * (1.0 + jnp.tanh(c * (x + 0.044715 * (x * x * x))))


def _rms(x, g):
    return x * lax.rsqrt(jnp.mean(x * x, axis=-1, keepdims=True) + EPS) * g


def _norm_mod(x, g, shift, scale):
    return _rms(x, g) * (1.0 + scale) + shift


def _layer_spec(shape, layer):
    zeros = (0,) * (len(shape) - 1)
    return pl.BlockSpec((None,) + tuple(shape[1:]), lambda *_: (layer,) + zeros)


def _full_spec(shape):
    zeros = (0,) * len(shape)
    return pl.BlockSpec(tuple(shape), lambda *_: zeros)


def _mod_kernel(c_ref, w_ref, b_ref, o_ref):
    c = c_ref[...]
    c_act = (c * _sigmoid(c)).astype(BF16)
    o_ref[...] = _dot(c_act, w_ref[...].astype(BF16)) + b_ref[...]


def _modulation(c, w_mod, b_mod):
    depth, d, n = w_mod.shape
    b = c.shape[0]
    return pl.pallas_call(
        _mod_kernel,
        grid=(depth, n // MOD_COLS),
        in_specs=[
            pl.BlockSpec((b, d), lambda l, j: (0, 0)),
            pl.BlockSpec((None, d, MOD_COLS), lambda l, j: (l, 0, j)),
            pl.BlockSpec((None, 1, MOD_COLS), lambda l, j: (l, 0, j)),
        ],
        out_specs=pl.BlockSpec((None, b, MOD_COLS), lambda l, j: (l, 0, j)),
        out_shape=jax.ShapeDtypeStruct((depth, b, n), F32),
        compiler_params=pltpu.CompilerParams(
            dimension_semantics=("arbitrary", "arbitrary"), vmem_limit_bytes=VMEM_LIMIT),
        name="modulation",
    )(c, w_mod, b_mod.reshape(depth, 1, n))


def _mix_in_kernel(x_ref, mod_ref, g_ref, win_ref, gsgu_ref, wsgu_ref, bsp_ref,
                   q_ref, k_ref, v_ref, kmean_ref, sgu_ref):
    tm = x_ref.shape[0]
    h = _norm_mod(x_ref[...], g_ref[...], mod_ref[0:1, :], mod_ref[1:2, :]).astype(BF16)

    qkv = _dot(h, win_ref[:, :QKV_WIDTH])
    q_ref[...] = (qkv[:, :ATTN_WIDTH] * Q_SCALE).astype(BF16)
    k = qkv[:, ATTN_WIDTH:2 * ATTN_WIDTH]
    k_ref[...] = k.astype(BF16)
    v_ref[...] = qkv[:, 2 * ATTN_WIDTH:].astype(BF16)
    for c in range(tm // MOBA_BLOCK):
        kmean_ref[c] = jnp.mean(k[c * MOBA_BLOCK:(c + 1) * MOBA_BLOCK], axis=0, keepdims=True)

    uv = _dot(h, win_ref[:, QKV_WIDTH:QKV_WIDTH + UV_WIDTH])
    gu = _gelu_tanh(uv[:, :SGU_WIDTH])
    vn = _rms(_gelu_tanh(uv[:, SGU_WIDTH:]), gsgu_ref[...]).astype(BF16)

    lane = lax.broadcasted_iota(jnp.int32, (SGU_CHUNK, LANES), 1)
    first = lane < SGU_GROUP_DIM
    wrow = lax.broadcasted_iota(jnp.int32, (SGU_CHUNK, 2 * SGU_CHUNK), 0)
    wcol = lax.broadcasted_iota(jnp.int32, (SGU_CHUNK, 2 * SGU_CHUNK), 1)
    causal = jnp.where(wcol >= SGU_CHUNK, wcol - SGU_CHUNK, wcol) <= wrow
    zero = jnp.zeros((), BF16)
    for gp in range(SGU_WIDTH // LANES):
        cols = slice(gp * LANES, (gp + 1) * LANES)
        w = jnp.concatenate([wsgu_ref[2 * gp], wsgu_ref[2 * gp + 1]], axis=1)
        w = jnp.where(causal, w, 0.0).astype(BF16)
        bias = bsp_ref[:, cols]
        for c in range(tm // SGU_CHUNK):
            rows = slice(c * SGU_CHUNK, (c + 1) * SGU_CHUNK)
            vb = vn[rows, cols]
            rhs = jnp.concatenate([jnp.where(first, vb, zero), jnp.where(first, zero, vb)], axis=0)
            mixed = _dot(w, rhs) + bias
            sgu_ref[rows, cols] = (gu[rows, cols] * mixed).astype(BF16)


def _mix_in(layer, x, mod, g_mix, w_in, g_sgu, w_sgu_s, b_sp):
    b, s, d = x.shape
    tm = TOKEN_TILE
    blocks_per_tile = tm // MOBA_BLOCK
    act = lambda width: jax.ShapeDtypeStruct((b, s, width), BF16)
    act_spec = lambda width: pl.BlockSpec((None, tm, width), lambda bi, i: (bi, i, 0))
    return pl.pallas_call(
        _mix_in_kernel,
        grid=(b, s // tm),
        in_specs=[
            pl.BlockSpec((None, tm, d), lambda bi, i: (bi, i, 0)),
            pl.BlockSpec((None, None, N_MOD, d), lambda bi, i: (layer, bi, 0, 0)),
            _layer_spec(g_mix.shape, layer),
            pl.BlockSpec((None, d, QKV_WIDTH + UV_WIDTH), lambda bi, i: (layer, 0, 0)),
            _layer_spec(g_sgu.shape, layer),
            _layer_spec(w_sgu_s.shape, layer),
            _layer_spec(b_sp.shape, layer),
        ],
        out_specs=[
            act_spec(ATTN_WIDTH), act_spec(ATTN_WIDTH), act_spec(ATTN_WIDTH),
            pl.BlockSpec((None, blocks_per_tile, 1, ATTN_WIDTH), lambda bi, i: (bi, i, 0, 0)),
            act_spec(SGU_WIDTH),
        ],
        out_shape=[
            act(ATTN_WIDTH), act(ATTN_WIDTH), act(ATTN_WIDTH),
            jax.ShapeDtypeStruct((b, s // MOBA_BLOCK, 1, ATTN_WIDTH), F32),
            act(SGU_WIDTH),
        ],
        compiler_params=pltpu.CompilerParams(
            dimension_semantics=("arbitrary", "arbitrary"), vmem_limit_bytes=VMEM_LIMIT),
        name="mix_in",
    )(x, mod, g_mix, w_in, g_sgu, w_sgu_s, b_sp)


def _moba_kernel(q_ref, k_ref, v_ref, km_ref, o_ref, vt_ref, ot_ref):
    nb = km_ref.shape[0]
    s_len = q_ref.shape[0]
    neg_inf = jnp.float32(-jnp.inf)
    blocks = [slice(j * MOBA_BLOCK, (j + 1) * MOBA_BLOCK) for j in range(nb)]

    for j in range(nb):
        vt = v_ref[blocks[j], :].astype(F32).T.astype(BF16)
        for h in range(HEADS_PER_STEP):
            vt_ref[h, :HEAD_DIM, blocks[j]] = vt[h * HEAD_DIM:(h + 1) * HEAD_DIM]
    for h in range(HEADS_PER_STEP):
        vt_ref[h, HEAD_DIM:, :] = jnp.ones((BF16_ROWS, s_len), BF16)

    q = q_ref[...]
    lane = lax.broadcasted_iota(jnp.int32, (s_len, LANES), 1)
    km = km_ref[...]
    km_hi = km.astype(BF16)
    km_lo = (km - km_hi.astype(F32)).astype(BF16)
    blk = lax.broadcasted_iota(jnp.int32, (nb, s_len), 0)
    q_blk = lax.broadcasted_iota(jnp.int32, (nb, s_len), 1) // MOBA_BLOCK
    kpos = lax.broadcasted_iota(jnp.int32, (MOBA_BLOCK, MOBA_BLOCK), 0)
    qpos = lax.broadcasted_iota(jnp.int32, (MOBA_BLOCK, MOBA_BLOCK), 1)
    causal = kpos <= qpos

    qh, bias = [], []
    for h in range(HEADS_PER_STEP):
        in_head = (lane >= h * HEAD_DIM) & (lane < (h + 1) * HEAD_DIM)
        qh.append(jnp.where(in_head, q, jnp.zeros((), BF16)))

        route = _dot_nt(km_hi, qh[h]) + _dot_nt(km_lo, qh[h])
        rank = jnp.zeros((nb, s_len), jnp.int32)
        for jp in range(nb):
            rj = route[jp:jp + 1, :]
            beats = (rj > route) | ((rj == route) & (blk > jp))
            rank = rank + jnp.where(beats & (q_blk > jp), 1, 0)
        bias.append(jnp.where((blk < q_blk) & (rank < MOBA_TOPK), 0.0, neg_inf))

    def scores(h, i):
        return _dot_nt(k_ref[:(i + 1) * MOBA_BLOCK, :], qh[h][blocks[i]])

    order = [(h, i) for i in range(nb) for h in range(HEADS_PER_STEP)]
    pending = [scores(*order[n]) for n in range(SCORE_LOOKAHEAD)]
    for n, (h, i) in enumerate(order):
        s = pending.pop(0)
        if n + SCORE_LOOKAHEAD < len(order):
            pending.append(scores(*order[n + SCORE_LOOKAHEAD]))
        parts = [s[blocks[j]] + bias[h][j:j + 1, blocks[i]] for j in range(i)]
        parts.append(jnp.where(causal, s[blocks[i]], neg_inf))
        s = jnp.concatenate(parts, axis=0)
        m = jnp.max(s, axis=0, keepdims=True)
        p = jnp.exp2(s - m).astype(BF16)
        pv = _dot(vt_ref[h, :, :(i + 1) * MOBA_BLOCK], p)
        ot_ref[h * HEAD_DIM:(h + 1) * HEAD_DIM, blocks[i]] = (
            pv[:HEAD_DIM] / pv[HEAD_DIM:HEAD_DIM + 1])

    for i in range(nb):
        o_ref[blocks[i], :] = ot_ref[:, blocks[i]].T.astype(o_ref.dtype)


def _moba(q, k, v, kmean):
    b, s, width = q.shape
    nb = s // MOBA_BLOCK
    slab = pl.BlockSpec((None, s, LANES), lambda bi, p: (bi, 0, p))
    return pl.pallas_call(
        _moba_kernel,
        grid=(b, width // LANES),
        in_specs=[slab, slab, slab, pl.BlockSpec((None, nb, LANES), lambda bi, p: (bi, 0, p))],
        out_specs=slab,
        out_shape=jax.ShapeDtypeStruct((b, s, width), BF16),
        scratch_shapes=[
            pltpu.VMEM((HEADS_PER_STEP, HEAD_DIM + BF16_ROWS, s), BF16),
            pltpu.VMEM((LANES, s), F32),
        ],
        compiler_params=pltpu.CompilerParams(
            dimension_semantics=("arbitrary", "arbitrary"), vmem_limit_bytes=VMEM_LIMIT),
        name="moba",
    )(q, k, v, kmean)


def _mix_out_kernel(x_ref, mod_ref, g_ref, attn_ref, sgu_ref, win_ref, wa_ref, ws_ref, wo_ref,
                    o_ref):
    d = x_ref.shape[1]
    gate_cols = QKV_WIDTH + UV_WIDTH
    x = x_ref[...]
    h = _norm_mod(x, g_ref[...], mod_ref[0:1, :], mod_ref[1:2, :]).astype(BF16)
    gates = _dot(h, win_ref[:, gate_cols:])
    ya = _dot(attn_ref[...], wa_ref[...])
    ys = _dot(sgu_ref[...], ws_ref[...])
    merged = _sigmoid(gates[:, :d]) * ya + _sigmoid(gates[:, d:]) * ys
    o_ref[...] = x + mod_ref[2:3, :] * _dot(merged.astype(BF16), wo_ref[...])


def _mix_out(layer, x, mod, g_mix, attn, sgu, w_in, w_attn_br, w_sgu_br, w_out):
    b, s, d = x.shape
    tm = TOKEN_TILE
    tile = lambda width: pl.BlockSpec((None, tm, width), lambda bi, i: (bi, i, 0))
    return pl.pallas_call(
        _mix_out_kernel,
        grid=(b, s // tm),
        in_specs=[
            tile(d),
            pl.BlockSpec((None, None, N_MOD, d), lambda bi, i: (layer, bi, 0, 0)),
            _layer_spec(g_mix.shape, layer),
            tile(attn.shape[2]),
            tile(sgu.shape[2]),
            _layer_spec(w_in.shape, layer),
            _layer_spec(w_attn_br.shape, layer),
            _layer_spec(w_sgu_br.shape, layer),
            _layer_spec(w_out.shape, layer),
        ],
        out_specs=tile(d),
        out_shape=jax.ShapeDtypeStruct(x.shape, F32),
        compiler_params=pltpu.CompilerParams(
            dimension_semantics=("arbitrary", "arbitrary"), vmem_limit_bytes=VMEM_LIMIT),
        name="mix_out",
    )(x, mod, g_mix, attn, sgu, w_in, w_attn_br, w_sgu_br, w_out)


def _conv_ffn_kernel(x_ref, mod_ref, g_ref, wup_ref, taps_ref, wdown_ref, gfin_ref, o_ref,
                     h_ref, gated_ref, tail_ref, *, final_norm):
    i = pl.program_id(1)
    tm = x_ref.shape[0]
    d_ff = wdown_ref.shape[0]
    tf = FF_CHUNK
    x = x_ref[...]
    h_ref[...] = _norm_mod(x, g_ref[...], mod_ref[3:4, :], mod_ref[4:5, :]).astype(BF16)
    head_row = lax.broadcasted_iota(jnp.int32, (SUBLANES, tf), 0)

    def conv_chunk(cols):
        up = _dot(h_ref[...], wup_ref[:, cols])
        prev = jnp.where(i > 0, tail_ref[:, cols], 0.0)
        tail_ref[:, cols] = up[tm - SUBLANES:, :]
        taps = taps_ref[:, cols]
        conv = taps[3:4, :] + taps[2:3, :] * up
        for back in (1, 2):
            shifted = pltpu.roll(up, back, 0)
            head = jnp.where(head_row < back, pltpu.roll(prev, back, 0), shifted[:SUBLANES])
            shifted = jnp.concatenate([head, shifted[SUBLANES:]], axis=0)
            conv = conv + taps[2 - back:3 - back, :] * shifted
        return conv

    for j in range(d_ff // tf):
        act = conv_chunk(slice(j * tf, (j + 1) * tf))
        lin = conv_chunk(slice(d_ff + j * tf, d_ff + (j + 1) * tf))
        gated_ref[:, j * tf:(j + 1) * tf] = (act * _sigmoid(act) * lin).astype(BF16)

    out = x + mod_ref[5:6, :] * _dot(gated_ref[...], wdown_ref[...])
    if final_norm:
        out = _rms(out, gfin_ref[...])
    o_ref[...] = out


def _conv_ffn(layer, x, mod, g_ffn, w_up, taps, w_down, g_final, final_norm):
    b, s, d = x.shape
    tm = TOKEN_TILE
    d_ff = w_down.shape[1]
    tile = pl.BlockSpec((None, tm, d), lambda bi, i: (bi, i, 0))
    return pl.pallas_call(
        functools.partial(_conv_ffn_kernel, final_norm=final_norm),
        grid=(b, s // tm),
        in_specs=[
            tile,
            pl.BlockSpec((None, None, N_MOD, d), lambda bi, i: (layer, bi, 0, 0)),
            _layer_spec(g_ffn.shape, layer),
            _layer_spec(w_up.shape, layer),
            _layer_spec(taps.shape, layer),
            _layer_spec(w_down.shape, layer),
            _full_spec(g_final.shape),
        ],
        out_specs=tile,
        out_shape=jax.ShapeDtypeStruct(x.shape, F32),
        scratch_shapes=[
            pltpu.VMEM((tm, d), BF16),
            pltpu.VMEM((tm, d_ff), BF16),
            pltpu.VMEM((SUBLANES, 2 * d_ff), F32),
        ],
        compiler_params=pltpu.CompilerParams(
            dimension_semantics=("arbitrary", "arbitrary"), vmem_limit_bytes=VMEM_LIMIT),
        name="conv_ffn",
    )(x, mod, g_ffn, w_up, taps, w_down, g_final)


def kernel(x, c, w_mod, b_mod, g_mix, w_in, g_sgu, w_sgu_s, b_sgu_s, w_attn_br, w_sgu_br, w_out,
           g_ffn, w_up, w_conv, b_conv, w_down, g_final):
    b, s, d = x.shape
    depth = w_mod.shape[0]
    d_ff = w_down.shape[1]
    assert s % TOKEN_TILE == 0 and TOKEN_TILE % MOBA_BLOCK == 0 and d_ff % FF_CHUNK == 0

    w_in_b, w_attn_b, w_sgu_b, w_out_b, w_up_b, w_down_b = (
        w.astype(BF16) for w in (w_in, w_attn_br, w_sgu_br, w_out, w_up, w_down))
    b_sp = jnp.repeat(jnp.swapaxes(b_sgu_s, 1, 2), SGU_GROUP_DIM, axis=2)
    taps = jnp.concatenate(
        [w_conv, b_conv[:, None, :],
         jnp.zeros((depth, SUBLANES - CONV_WIDTH - 1, 2 * d_ff), F32)], axis=1)
    g_mix3, g_sgu3, g_ffn3 = (g.reshape(depth, 1, -1) for g in (g_mix, g_sgu, g_ffn))
    g_fin = g_final.reshape(1, d)

    mod = _modulation(c, w_mod, b_mod).reshape(depth, b, N_MOD, d)
    for l in range(depth):
        q, k, v, kmean, sgu = _mix_in(l, x, mod, g_mix3, w_in_b, g_sgu3, w_sgu_s, b_sp)
        attn = _moba(q, k, v, kmean.reshape(b, s // MOBA_BLOCK, ATTN_WIDTH))
        x = _mix_out(l, x, mod, g_mix3, attn, sgu, w_in_b, w_attn_b, w_sgu_b, w_out_b)
        x = _conv_ffn(l, x, mod, g_ffn3, w_up_b, taps, w_down_b, g_fin,
                      final_norm=(l == depth - 1))
    return x
```

```python
import functools
import math

import jax
import jax.numpy as jnp
from jax import lax
from jax.experimental import pallas as pl
from jax.experimental.pallas import tpu as pltpu

F32 = jnp.float32
BF16 = jnp.bfloat16

N_HEADS = 8
HEAD_DIM = 64
ATTN_WIDTH = N_HEADS * HEAD_DIM
MOBA_BLOCK = 256
MOBA_TOPK = 3
SGU_GROUPS = 8
SGU_GROUP_DIM = 64
SGU_WIDTH = SGU_GROUPS * SGU_GROUP_DIM
SGU_CHUNK = 128
CONV_WIDTH = 3
N_MOD = 6
EPS = 1e-6
QKV_WIDTH = 3 * ATTN_WIDTH
UV_WIDTH = 2 * SGU_WIDTH

LANES = 128
SUBLANES = 8
BF16_ROWS = 16
HEADS_PER_STEP = LANES // HEAD_DIM
VMEM_LIMIT = 56 * 1024 * 1024

TOKEN_TILE = 512
FF_CHUNK = 256
MOD_COLS = 2048
SCORE_LOOKAHEAD = 3
Q_SCALE = HEAD_DIM ** -0.5 * math.log2(math.e)


def _dot(a, b):
    return jnp.dot(a, b, preferred_element_type=F32)


def _dot_nt(a, b):
    return lax.dot_general(a, b, (((1,), (1,)), ((), ())), preferred_element_type=F32)


def _sigmoid(x):
    return 0.5 + 0.5 * jnp.tanh(0.5 * x)


def _silu(x):
    half = 0.5 * x
    return half + half * jnp.tanh(half)


def _gelu_tanh(x):
    c = 0.7978845608028654
    return 0.5 * x * (1.0 + jnp.tanh(c * (x + 0.044715 * (x * x * x))))


def _rms(x, g):
    return x * lax.rsqrt(jnp.mean(x * x, axis=-1, keepdims=True) + EPS) * g


def _norm_mod(x, g, shift, scale):
    return _rms(x, g * (1.0 + scale)) + shift


def _layer_spec(shape, layer):
    zeros = (0,) * (len(shape) - 1)
    return pl.BlockSpec((None,) + tuple(shape[1:]), lambda *_: (layer,) + zeros)


def _full_spec(shape):
    zeros = (0,) * len(shape)
    return pl.BlockSpec(tuple(shape), lambda *_: zeros)


def _mod_kernel(c_ref, w_ref, b_ref, o_ref):
    c = c_ref[...]
    c_act = _silu(c).astype(BF16)
    o_ref[...] = _dot(c_act, w_ref[...].astype(BF16)) + b_ref[...]


def _modulation(c, w_mod, b_mod):
    depth, d, n = w_mod.shape
    b = c.shape[0]
    return pl.pallas_call(
        _mod_kernel,
        grid=(depth, n // MOD_COLS),
        in_specs=[
            pl.BlockSpec((b, d), lambda l, j: (0, 0)),
            pl.BlockSpec((None, d, MOD_COLS), lambda l, j: (l, 0, j)),
            pl.BlockSpec((None, 1, MOD_COLS), lambda l, j: (l, 0, j)),
        ],
        out_specs=pl.BlockSpec((None, b, MOD_COLS), lambda l, j: (l, 0, j)),
        out_shape=jax.ShapeDtypeStruct((depth, b, n), F32),
        compiler_params=pltpu.CompilerParams(
            dimension_semantics=("arbitrary", "arbitrary"), vmem_limit_bytes=VMEM_LIMIT),
        name="modulation",
    )(c, w_mod, b_mod.reshape(depth, 1, n))


def _mix_in_kernel(x_ref, mod_ref, g_ref, win_ref, gsgu_ref, wsgu_ref, bsp_ref,
                   q_ref, k_ref, v_ref, kmean_ref, sgu_ref):
    tm = x_ref.shape[0]
    h = _norm_mod(x_ref[...], g_ref[...], mod_ref[0:1, :], mod_ref[1:2, :]).astype(BF16)

    uv = _dot(h, win_ref[:, QKV_WIDTH:QKV_WIDTH + UV_WIDTH])
    qk = _dot(h, win_ref[:, :2 * ATTN_WIDTH])
    q_ref[...] = (qk[:, :ATTN_WIDTH] * Q_SCALE).astype(BF16)
    k = qk[:, ATTN_WIDTH:]
    k_ref[...] = k.astype(BF16)
    for c in range(tm // MOBA_BLOCK):
        kmean_ref[c] = jnp.mean(k[c * MOBA_BLOCK:(c + 1) * MOBA_BLOCK], axis=0, keepdims=True)

    gu = _gelu_tanh(uv[:, :SGU_WIDTH])
    vn = _rms(_gelu_tanh(uv[:, SGU_WIDTH:]), gsgu_ref[...]).astype(BF16)

    lane = lax.broadcasted_iota(jnp.int32, (SGU_CHUNK, LANES), 1)
    first = lane < SGU_GROUP_DIM
    wrow = lax.broadcasted_iota(jnp.int32, (SGU_CHUNK, 2 * SGU_CHUNK), 0)
    wcol = lax.broadcasted_iota(jnp.int32, (SGU_CHUNK, 2 * SGU_CHUNK), 1)
    causal = jnp.where(wcol >= SGU_CHUNK, wcol - SGU_CHUNK, wcol) <= wrow
    zero = jnp.zeros((), BF16)
    for gp in range(SGU_WIDTH // LANES):
        cols = slice(gp * LANES, (gp + 1) * LANES)
        w = jnp.concatenate([wsgu_ref[2 * gp], wsgu_ref[2 * gp + 1]], axis=1)
        w = jnp.where(causal, w, 0.0).astype(BF16)
        bias = bsp_ref[:, cols]
        for c in range(tm // SGU_CHUNK):
            rows = slice(c * SGU_CHUNK, (c + 1) * SGU_CHUNK)
            vb = vn[rows, cols]
            rhs = jnp.concatenate([jnp.where(first, vb, zero), jnp.where(first, zero, vb)], axis=0)
            mixed = _dot(w, rhs) + bias
            sgu_ref[rows, cols] = (gu[rows, cols] * mixed).astype(BF16)

    v_ref[...] = _dot(h, win_ref[:, 2 * ATTN_WIDTH:QKV_WIDTH]).astype(BF16)


def _mix_in(layer, x, mod, g_mix, w_in, g_sgu, w_sgu_s, b_sp):
    b, s, d = x.shape
    tm = TOKEN_TILE
    blocks_per_tile = tm // MOBA_BLOCK
    act = lambda width: jax.ShapeDtypeStruct((b, s, width), BF16)
    act_spec = lambda width: pl.BlockSpec((None, tm, width), lambda bi, i: (bi, i, 0))
    return pl.pallas_call(
        _mix_in_kernel,
        grid=(b, s // tm),
        in_specs=[
            pl.BlockSpec((None, tm, d), lambda bi, i: (bi, i, 0)),
            pl.BlockSpec((None, None, N_MOD, d), lambda bi, i: (layer, bi, 0, 0)),
            _layer_spec(g_mix.shape, layer),
            pl.BlockSpec((None, d, QKV_WIDTH + UV_WIDTH), lambda bi, i: (layer, 0, 0)),
            _layer_spec(g_sgu.shape, layer),
            _layer_spec(w_sgu_s.shape, layer),
            _layer_spec(b_sp.shape, layer),
        ],
        out_specs=[
            act_spec(ATTN_WIDTH), act_spec(ATTN_WIDTH), act_spec(ATTN_WIDTH),
            pl.BlockSpec((None, blocks_per_tile, 1, ATTN_WIDTH), lambda bi, i: (bi, i, 0, 0)),
            act_spec(SGU_WIDTH),
        ],
        out_shape=[
            act(ATTN_WIDTH), act(ATTN_WIDTH), act(ATTN_WIDTH),
            jax.ShapeDtypeStruct((b, s // MOBA_BLOCK, 1, ATTN_WIDTH), F32),
            act(SGU_WIDTH),
        ],
        compiler_params=pltpu.CompilerParams(
            dimension_semantics=("arbitrary", "arbitrary"), vmem_limit_bytes=VMEM_LIMIT),
        name="mix_in",
    )(x, mod, g_mix, w_in, g_sgu, w_sgu_s, b_sp)


def _moba_kernel(q_ref, k_ref, v_ref, km_ref, o_ref, vt_ref, ot_ref):
    nb = km_ref.shape[0]
    s_len = q_ref.shape[0]
    neg_inf = jnp.float32(-jnp.inf)
    blocks = [slice(j * MOBA_BLOCK, (j + 1) * MOBA_BLOCK) for j in range(nb)]

    q = q_ref[...]
    lane = lax.broadcasted_iota(jnp.int32, (s_len, LANES), 1)
    qh = []
    for h in range(HEADS_PER_STEP):
        in_head = (lane >= h * HEAD_DIM) & (lane < (h + 1) * HEAD_DIM)
        qh.append(jnp.where(in_head, q, jnp.zeros((), BF16)))

    def scores(h, i):
        return _dot_nt(k_ref[:(i + 1) * MOBA_BLOCK, :], qh[h][blocks[i]])

    order = [(h, i) for i in range(nb) for h in range(HEADS_PER_STEP)]
    pending = [scores(*order[n]) for n in range(SCORE_LOOKAHEAD)]

    for j in range(nb):
        vt = v_ref[blocks[j], :].astype(F32).T.astype(BF16)
        for h in range(HEADS_PER_STEP):
            vt_ref[h, :HEAD_DIM, blocks[j]] = vt[h * HEAD_DIM:(h + 1) * HEAD_DIM]
    for h in range(HEADS_PER_STEP):
        vt_ref[h, HEAD_DIM:, :] = jnp.ones((BF16_ROWS, s_len), BF16)

    km = km_ref[...]
    km_hi = km.astype(BF16)
    km_lo = (km - km_hi.astype(F32)).astype(BF16)
    blk = lax.broadcasted_iota(jnp.int32, (nb, s_len), 0)
    q_blk = lax.broadcasted_iota(jnp.int32, (nb, s_len), 1) // MOBA_BLOCK
    kpos = lax.broadcasted_iota(jnp.int32, (MOBA_BLOCK, MOBA_BLOCK), 0)
    qpos = lax.broadcasted_iota(jnp.int32, (MOBA_BLOCK, MOBA_BLOCK), 1)
    causal = kpos <= qpos

    bias = []
    for h in range(HEADS_PER_STEP):
        route = _dot_nt(km_hi, qh[h]) + _dot_nt(km_lo, qh[h])
        rank = jnp.zeros((nb, s_len), jnp.int32)
        for jp in range(nb):
            rj = route[jp:jp + 1, :]
            beats = (rj > route) | ((rj == route) & (blk > jp))
            rank = rank + jnp.where(beats & (q_blk > jp), 1, 0)
        bias.append(jnp.where((blk < q_blk) & (rank < MOBA_TOPK), 0.0, neg_inf))

    for n, (h, i) in enumerate(order):
        s = pending.pop(0)
        if n + SCORE_LOOKAHEAD < len(order):
            pending.append(scores(*order[n + SCORE_LOOKAHEAD]))
        parts = [s[blocks[j]] + bias[h][j:j + 1, blocks[i]] for j in range(i)]
        parts.append(jnp.where(causal, s[blocks[i]], neg_inf))
        s = jnp.concatenate(parts, axis=0)
        m = jnp.max(s, axis=0, keepdims=True)
        p = jnp.exp2(s - m).astype(BF16)
        pv = _dot(vt_ref[h, :, :(i + 1) * MOBA_BLOCK], p)
        ot_ref[h * HEAD_DIM:(h + 1) * HEAD_DIM, blocks[i]] = (
            pv[:HEAD_DIM] / pv[HEAD_DIM:HEAD_DIM + 1])
        if h == HEADS_PER_STEP - 1:
            o_ref[blocks[i], :] = ot_ref[:, blocks[i]].T.astype(o_ref.dtype)


def _moba(q, k, v, kmean):
    b, s, width = q.shape
    nb = s // MOBA_BLOCK
    slab = pl.BlockSpec((None, s, LANES), lambda bi, p: (bi, 0, p))
    return pl.pallas_call(
        _moba_kernel,
        grid=(b, width // LANES),
        in_specs=[slab, slab, slab, pl.BlockSpec((None, nb, LANES), lambda bi, p: (bi, 0, p))],
        out_specs=slab,
        out_shape=jax.ShapeDtypeStruct((b, s, width), BF16),
        scratch_shapes=[
            pltpu.VMEM((HEADS_PER_STEP, HEAD_DIM + BF16_ROWS, s), BF16),
            pltpu.VMEM((LANES, s), F32),
        ],
        compiler_params=pltpu.CompilerParams(
            dimension_semantics=("arbitrary", "arbitrary"), vmem_limit_bytes=VMEM_LIMIT),
        name="moba",
    )(q, k, v, kmean)


def _mix_out_kernel(x_ref, mod_ref, g_ref, attn_ref, sgu_ref, win_ref, wa_ref, ws_ref, wo_ref,
                    o_ref):
    d = x_ref.shape[1]
    gate_cols = QKV_WIDTH + UV_WIDTH
    x = x_ref[...]
    h = _norm_mod(x, g_ref[...], mod_ref[0:1, :], mod_ref[1:2, :]).astype(BF16)
    gates = _dot(h, win_ref[:, gate_cols:])
    ya = _dot(attn_ref[...], wa_ref[...])
    ys = _dot(sgu_ref[...], ws_ref[...])
    merged = _sigmoid(gates[:, :d]) * ya + _sigmoid(gates[:, d:]) * ys
    o_ref[...] = x + mod_ref[2:3, :] * _dot(merged.astype(BF16), wo_ref[...])


def _mix_out(layer, x, mod, g_mix, attn, sgu, w_in, w_attn_br, w_sgu_br, w_out):
    b, s, d = x.shape
    tm = TOKEN_TILE
    tile = lambda width: pl.BlockSpec((None, tm, width), lambda bi, i: (bi, i, 0))
    return pl.pallas_call(
        _mix_out_kernel,
        grid=(b, s // tm),
        in_specs=[
            tile(d),
            pl.BlockSpec((None, None, N_MOD, d), lambda bi, i: (layer, bi, 0, 0)),
            _layer_spec(g_mix.shape, layer),
            tile(attn.shape[2]),
            tile(sgu.shape[2]),
            _layer_spec(w_in.shape, layer),
            _layer_spec(w_attn_br.shape, layer),
            _layer_spec(w_sgu_br.shape, layer),
            _layer_spec(w_out.shape, layer),
        ],
        out_specs=tile(d),
        out_shape=jax.ShapeDtypeStruct(x.shape, F32),
        compiler_params=pltpu.CompilerParams(
            dimension_semantics=("arbitrary", "arbitrary"), vmem_limit_bytes=VMEM_LIMIT),
        name="mix_out",
    )(x, mod, g_mix, attn, sgu, w_in, w_attn_br, w_sgu_br, w_out)


def _conv_ffn_kernel(x_ref, mod_ref, g_ref, wup_ref, taps_ref, wdown_ref, gfin_ref, o_ref,
                     h_ref, gated_ref, tail_ref, wina_ref, winl_ref, *, final_norm):
    i = pl.program_id(1)
    tm = x_ref.shape[0]
    d_ff = wdown_ref.shape[0]
    tf = FF_CHUNK
    x = x_ref[...]
    h_ref[...] = _norm_mod(x, g_ref[...], mod_ref[3:4, :], mod_ref[4:5, :]).astype(BF16)

    def conv_chunk(cols, win_ref):
        up = _dot(h_ref[...], wup_ref[:, cols])
        win_ref[:SUBLANES, :] = jnp.where(i > 0, tail_ref[:, cols], 0.0)
        win_ref[SUBLANES:, :] = up
        tail_ref[:, cols] = up[tm - SUBLANES:, :]
        taps = taps_ref[:, cols]
        conv = taps[3:4, :] + taps[2:3, :] * up
        for back in (1, 2):
            conv = conv + taps[2 - back:3 - back, :] * win_ref[pl.ds(SUBLANES - back, tm), :]
        return conv

    for j in range(d_ff // tf):
        act = conv_chunk(slice(j * tf, (j + 1) * tf), wina_ref)
        lin = conv_chunk(slice(d_ff + j * tf, d_ff + (j + 1) * tf), winl_ref)
        gated_ref[:, j * tf:(j + 1) * tf] = (_silu(act) * lin).astype(BF16)

    out = x + mod_ref[5:6, :] * _dot(gated_ref[...], wdown_ref[...])
    if final_norm:
        out = _rms(out, gfin_ref[...])
    o_ref[...] = out


def _conv_ffn(layer, x, mod, g_ffn, w_up, taps, w_down, g_final, final_norm):
    b, s, d = x.shape
    tm = TOKEN_TILE
    d_ff = w_down.shape[1]
    tile = pl.BlockSpec((None, tm, d), lambda bi, i: (bi, i, 0))
    return pl.pallas_call(
        functools.partial(_conv_ffn_kernel, final_norm=final_norm),
        grid=(b, s // tm),
        in_specs=[
            tile,
            pl.BlockSpec((None, None, N_MOD, d), lambda bi, i: (layer, bi, 0, 0)),
            _layer_spec(g_ffn.shape, layer),
            _layer_spec(w_up.shape, layer),
            _layer_spec(taps.shape, layer),
            _layer_spec(w_down.shape, layer),
            _full_spec(g_final.shape),
        ],
        out_specs=tile,
        out_shape=jax.ShapeDtypeStruct(x.shape, F32),
        scratch_shapes=[
            pltpu.VMEM((tm, d), BF16),
            pltpu.VMEM((tm, d_ff), BF16),
            pltpu.VMEM((SUBLANES, 2 * d_ff), F32),
            pltpu.VMEM((SUBLANES + tm, FF_CHUNK), F32),
            pltpu.VMEM((SUBLANES + tm, FF_CHUNK), F32),
        ],
        compiler_params=pltpu.CompilerParams(
            dimension_semantics=("arbitrary", "arbitrary"), vmem_limit_bytes=VMEM_LIMIT),
        name="conv_ffn",
    )(x, mod, g_ffn, w_up, taps, w_down, g_final)


def kernel(x, c, w_mod, b_mod, g_mix, w_in, g_sgu, w_sgu_s, b_sgu_s, w_attn_br, w_sgu_br, w_out,
           g_ffn, w_up, w_conv, b_conv, w_down, g_final):
    b, s, d = x.shape
    depth = w_mod.shape[0]
    d_ff = w_down.shape[1]
    assert s % TOKEN_TILE == 0 and TOKEN_TILE % MOBA_BLOCK == 0 and d_ff % FF_CHUNK == 0

    w_in_b, w_attn_b, w_sgu_b, w_out_b, w_up_b, w_down_b = (
        w.astype(BF16) for w in (w_in, w_attn_br, w_sgu_br, w_out, w_up, w_down))
    b_sp = jnp.repeat(jnp.swapaxes(b_sgu_s, 1, 2), SGU_GROUP_DIM, axis=2)
    taps = jnp.concatenate(
        [w_conv, b_conv[:, None, :],
         jnp.zeros((depth, SUBLANES - CONV_WIDTH - 1, 2 * d_ff), F32)], axis=1)
    g_mix3, g_sgu3, g_ffn3 = (g.reshape(depth, 1, -1) for g in (g_mix, g_sgu, g_ffn))
    g_fin = g_final.reshape(1, d)

    mod = _modulation(c, w_mod, b_mod).reshape(depth, b, N_MOD, d)
    for l in range(depth):
        q, k, v, kmean, sgu = _mix_in(l, x, mod, g_mix3, w_in_b, g_sgu3, w_sgu_s, b_sp)
        attn = _moba(q, k, v, kmean.reshape(b, s // MOBA_BLOCK, ATTN_WIDTH))
        x = _mix_out(l, x, mod, g_mix3, attn, sgu, w_in_b, w_attn_b, w_sgu_b, w_out_b)
        x = _conv_ffn(l, x, mod, g_ffn3, w_up_b, taps, w_down_b, g_fin,
                      final_norm=(l == depth - 1))
    return x
```

```python
import functools
import math

import jax
import jax.numpy as jnp
from jax import lax
from jax.experimental import pallas as pl
from jax.experimental.pallas import tpu as pltpu

F32 = jnp.float32
BF16 = jnp.bfloat16

N_HEADS = 8
HEAD_DIM = 64
ATTN_WIDTH = N_HEADS * HEAD_DIM
MOBA_BLOCK = 256
MOBA_TOPK = 3
SGU_GROUPS = 8
SGU_GROUP_DIM = 64
SGU_WIDTH = SGU_GROUPS * SGU_GROUP_DIM
SGU_CHUNK = 128
CONV_WIDTH = 3
N_MOD = 6
EPS = 1e-6
QKV_WIDTH = 3 * ATTN_WIDTH
UV_WIDTH = 2 * SGU_WIDTH

LANES = 128
SUBLANES = 8
BF16_ROWS = 16
HEADS_PER_STEP = LANES // HEAD_DIM
VMEM_LIMIT = 56 * 1024 * 1024

TOKEN_TILE = 512
FF_CHUNK = 256
MOD_COLS = 2048
SCORE_LOOKAHEAD = 3
Q_SCALE = HEAD_DIM ** -0.5 * math.log2(math.e)


def _dot(a, b):
    return jnp.dot(a, b, preferred_element_type=F32)


def _dot_nt(a, b):
    return lax.dot_general(a, b, (((1,), (1,)), ((), ())), preferred_element_type=F32)


def _sigmoid(x):
    return 0.5 + 0.5 * jnp.tanh(0.5 * x)


def _silu(x):
    half = 0.5 * x
    return half + half * jnp.tanh(half)


def _gelu_tanh(x):
    c = 0.7978845608028654
    return 0.5 * x * (1.0 + jnp.tanh(c * (x + 0.044715 * (x * x * x))))


def _rms(x, g):
    return x * lax.rsqrt(jnp.mean(x * x, axis=-1, keepdims=True) + EPS) * g


def _norm_mod(x, g, shift, scale):
    return _rms(x, g * (1.0 + scale)) + shift


def _layer_spec(shape, layer):
    zeros = (0,) * (len(shape) - 1)
    return pl.BlockSpec((None,) + tuple(shape[1:]), lambda *_: (layer,) + zeros)


def _full_spec(shape):
    zeros = (0,) * len(shape)
    return pl.BlockSpec(tuple(shape), lambda *_: zeros)


def _mod_kernel(c_ref, w_ref, b_ref, o_ref):
    c = c_ref[...]
    c_act = _silu(c).astype(BF16)
    o_ref[...] = _dot(c_act, w_ref[...].astype(BF16)) + b_ref[...]


def _modulation(c, w_mod, b_mod):
    depth, d, n = w_mod.shape
    b = c.shape[0]
    return pl.pallas_call(
        _mod_kernel,
        grid=(depth, n // MOD_COLS),
        in_specs=[
            pl.BlockSpec((b, d), lambda l, j: (0, 0)),
            pl.BlockSpec((None, d, MOD_COLS), lambda l, j: (l, 0, j)),
            pl.BlockSpec((None, 1, MOD_COLS), lambda l, j: (l, 0, j)),
        ],
        out_specs=pl.BlockSpec((None, b, MOD_COLS), lambda l, j: (l, 0, j)),
        out_shape=jax.ShapeDtypeStruct((depth, b, n), F32),
        compiler_params=pltpu.CompilerParams(
            dimension_semantics=("arbitrary", "arbitrary"), vmem_limit_bytes=VMEM_LIMIT),
        name="modulation",
    )(c, w_mod, b_mod.reshape(depth, 1, n))


def _mix_in_kernel(x_ref, mod_ref, g_ref, win_ref, gsgu_ref, wsgu_ref, bsp_ref,
                   q_ref, k_ref, v_ref, kmean_ref, sgu_ref):
    tm = x_ref.shape[0]
    h = _norm_mod(x_ref[...], g_ref[...], mod_ref[0:1, :], mod_ref[1:2, :]).astype(BF16)

    uv = _dot(h, win_ref[:, QKV_WIDTH:QKV_WIDTH + UV_WIDTH])
    qk = _dot(h, win_ref[:, :2 * ATTN_WIDTH])
    q_ref[...] = (qk[:, :ATTN_WIDTH] * Q_SCALE).astype(BF16)
    k = qk[:, ATTN_WIDTH:]
    k_ref[...] = k.astype(BF16)
    for c in range(tm // MOBA_BLOCK):
        kmean_ref[c] = jnp.mean(k[c * MOBA_BLOCK:(c + 1) * MOBA_BLOCK], axis=0, keepdims=True)

    gu = _gelu_tanh(uv[:, :SGU_WIDTH])
    vn = _rms(_gelu_tanh(uv[:, SGU_WIDTH:]), gsgu_ref[...]).astype(BF16)

    lane = lax.broadcasted_iota(jnp.int32, (SGU_CHUNK, LANES), 1)
    first = lane < SGU_GROUP_DIM
    wrow = lax.broadcasted_iota(jnp.int32, (SGU_CHUNK, 2 * SGU_CHUNK), 0)
    wcol = lax.broadcasted_iota(jnp.int32, (SGU_CHUNK, 2 * SGU_CHUNK), 1)
    causal = jnp.where(wcol >= SGU_CHUNK, wcol - SGU_CHUNK, wcol) <= wrow
    zero = jnp.zeros((), BF16)
    for gp in range(SGU_WIDTH // LANES):
        cols = slice(gp * LANES, (gp + 1) * LANES)
        w = jnp.concatenate([wsgu_ref[2 * gp], wsgu_ref[2 * gp + 1]], axis=1)
        w = jnp.where(causal, w, 0.0).astype(BF16)
        bias = bsp_ref[:, cols]
        for c in range(tm // SGU_CHUNK):
            rows = slice(c * SGU_CHUNK, (c + 1) * SGU_CHUNK)
            vb = vn[rows, cols]
            rhs = jnp.concatenate([jnp.where(first, vb, zero), jnp.where(first, zero, vb)], axis=0)
            mixed = _dot(w, rhs) + bias
            sgu_ref[rows, cols] = (gu[rows, cols] * mixed).astype(BF16)

    v_ref[...] = _dot(h, win_ref[:, 2 * ATTN_WIDTH:QKV_WIDTH]).astype(BF16)


def _mix_in(layer, x, mod, g_mix, w_in, g_sgu, w_sgu_s, b_sp):
    b, s, d = x.shape
    tm = TOKEN_TILE
    blocks_per_tile = tm // MOBA_BLOCK
    act = lambda width: jax.ShapeDtypeStruct((b, s, width), BF16)
    act_spec = lambda width: pl.BlockSpec((None, tm, width), lambda bi, i: (bi, i, 0))
    return pl.pallas_call(
        _mix_in_kernel,
        grid=(b, s // tm),
        in_specs=[
            pl.BlockSpec((None, tm, d), lambda bi, i: (bi, i, 0)),
            pl.BlockSpec((None, None, N_MOD, d), lambda bi, i: (layer, bi, 0, 0)),
            _layer_spec(g_mix.shape, layer),
            pl.BlockSpec((d, QKV_WIDTH + UV_WIDTH), lambda bi, i: (0, 0)),
            _layer_spec(g_sgu.shape, layer),
            _layer_spec(w_sgu_s.shape, layer),
            _layer_spec(b_sp.shape, layer),
        ],
        out_specs=[
            act_spec(ATTN_WIDTH), act_spec(ATTN_WIDTH), act_spec(ATTN_WIDTH),
            pl.BlockSpec((None, blocks_per_tile, 1, ATTN_WIDTH), lambda bi, i: (bi, i, 0, 0)),
            act_spec(SGU_WIDTH),
        ],
        out_shape=[
            act(ATTN_WIDTH), act(ATTN_WIDTH), act(ATTN_WIDTH),
            jax.ShapeDtypeStruct((b, s // MOBA_BLOCK, 1, ATTN_WIDTH), F32),
            act(SGU_WIDTH),
        ],
        compiler_params=pltpu.CompilerParams(
            dimension_semantics=("arbitrary", "arbitrary"), vmem_limit_bytes=VMEM_LIMIT),
        name="mix_in",
    )(x, mod, g_mix, w_in, g_sgu, w_sgu_s, b_sp)


def _moba_kernel(q_ref, k_ref, v_ref, km_ref, o_ref, vt_ref, ot_ref):
    nb = km_ref.shape[0]
    s_len = q_ref.shape[0]
    neg_inf = jnp.float32(-jnp.inf)
    blocks = [slice(j * MOBA_BLOCK, (j + 1) * MOBA_BLOCK) for j in range(nb)]

    q = q_ref[...]
    lane = lax.broadcasted_iota(jnp.int32, (s_len, LANES), 1)
    qh = []
    for h in range(HEADS_PER_STEP):
        in_head = (lane >= h * HEAD_DIM) & (lane < (h + 1) * HEAD_DIM)
        qh.append(jnp.where(in_head, q, jnp.zeros((), BF16)))

    def scores(h, i):
        return _dot_nt(k_ref[:(i + 1) * MOBA_BLOCK, :], qh[h][blocks[i]])

    order = [(h, i) for i in range(nb) for h in range(HEADS_PER_STEP)]
    pending = [scores(*order[n]) for n in range(SCORE_LOOKAHEAD)]

    for j in range(nb):
        vt = v_ref[blocks[j], :].astype(F32).T.astype(BF16)
        for h in range(HEADS_PER_STEP):
            vt_ref[h, :HEAD_DIM, blocks[j]] = vt[h * HEAD_DIM:(h + 1) * HEAD_DIM]
    for h in range(HEADS_PER_STEP):
        vt_ref[h, HEAD_DIM:, :] = jnp.ones((BF16_ROWS, s_len), BF16)

    km = km_ref[...]
    km_hi = km.astype(BF16)
    km_lo = (km - km_hi.astype(F32)).astype(BF16)
    blk = lax.broadcasted_iota(jnp.int32, (nb, s_len), 0)
    q_blk = lax.broadcasted_iota(jnp.int32, (nb, s_len), 1) // MOBA_BLOCK
    kpos = lax.broadcasted_iota(jnp.int32, (MOBA_BLOCK, MOBA_BLOCK), 0)
    qpos = lax.broadcasted_iota(jnp.int32, (MOBA_BLOCK, MOBA_BLOCK), 1)
    causal = kpos <= qpos

    bias = []
    for h in range(HEADS_PER_STEP):
        route = _dot_nt(km_hi, qh[h]) + _dot_nt(km_lo, qh[h])
        rank = jnp.zeros((nb, s_len), jnp.int32)
        for jp in range(nb):
            rj = route[jp:jp + 1, :]
            beats = (rj > route) | ((rj == route) & (blk > jp))
            rank = rank + jnp.where(beats & (q_blk > jp), 1, 0)
        bias.append(jnp.where((blk < q_blk) & (rank < MOBA_TOPK), 0.0, neg_inf))

    for n, (h, i) in enumerate(order):
        s = pending.pop(0)
        if n + SCORE_LOOKAHEAD < len(order):
            pending.append(scores(*order[n + SCORE_LOOKAHEAD]))
        parts = [s[blocks[j]] + bias[h][j:j + 1, blocks[i]] for j in range(i)]
        parts.append(jnp.where(causal, s[blocks[i]], neg_inf))
        s = jnp.concatenate(parts, axis=0)
        m = jnp.max(s, axis=0, keepdims=True)
        p = jnp.exp2(s - m).astype(BF16)
        pv = _dot(vt_ref[h, :, :(i + 1) * MOBA_BLOCK], p)
        ot_ref[h * HEAD_DIM:(h + 1) * HEAD_DIM, blocks[i]] = (
            pv[:HEAD_DIM] / pv[HEAD_DIM:HEAD_DIM + 1])
        if h == HEADS_PER_STEP - 1:
            o_ref[blocks[i], :] = ot_ref[:, blocks[i]].T.astype(o_ref.dtype)


def _moba(q, k, v, kmean):
    b, s, width = q.shape
    nb = s // MOBA_BLOCK
    slab = pl.BlockSpec((None, s, LANES), lambda bi, p: (bi, 0, p))
    return pl.pallas_call(
        _moba_kernel,
        grid=(b, width // LANES),
        in_specs=[slab, slab, slab, pl.BlockSpec((None, nb, LANES), lambda bi, p: (bi, 0, p))],
        out_specs=slab,
        out_shape=jax.ShapeDtypeStruct((b, s, width), BF16),
        scratch_shapes=[
            pltpu.VMEM((HEADS_PER_STEP, HEAD_DIM + BF16_ROWS, s), BF16),
            pltpu.VMEM((LANES, s), F32),
        ],
        compiler_params=pltpu.CompilerParams(
            dimension_semantics=("arbitrary", "arbitrary"), vmem_limit_bytes=VMEM_LIMIT),
        name="moba",
    )(q, k, v, kmean)


def _mix_out_kernel(x_ref, mod_ref, g_ref, attn_ref, sgu_ref, win_ref, wa_ref, ws_ref, wo_ref,
                    o_ref):
    d = x_ref.shape[1]
    gate_cols = QKV_WIDTH + UV_WIDTH
    x = x_ref[...]
    h = _norm_mod(x, g_ref[...], mod_ref[0:1, :], mod_ref[1:2, :]).astype(BF16)
    gates = _dot(h, win_ref[:, gate_cols:])
    ya = _dot(attn_ref[...], wa_ref[...])
    ys = _dot(sgu_ref[...], ws_ref[...])
    merged = _sigmoid(gates[:, :d]) * ya + _sigmoid(gates[:, d:]) * ys
    o_ref[...] = x + mod_ref[2:3, :] * _dot(merged.astype(BF16), wo_ref[...])


def _mix_out(layer, x, mod, g_mix, attn, sgu, w_in, w_attn_br, w_sgu_br, w_out):
    b, s, d = x.shape
    tm = TOKEN_TILE
    tile = lambda width: pl.BlockSpec((None, tm, width), lambda bi, i: (bi, i, 0))
    return pl.pallas_call(
        _mix_out_kernel,
        grid=(b, s // tm),
        in_specs=[
            tile(d),
            pl.BlockSpec((None, None, N_MOD, d), lambda bi, i: (layer, bi, 0, 0)),
            _layer_spec(g_mix.shape, layer),
            tile(attn.shape[2]),
            tile(sgu.shape[2]),
            _full_spec(w_in.shape),
            _full_spec(w_attn_br.shape),
            _full_spec(w_sgu_br.shape),
            _full_spec(w_out.shape),
        ],
        out_specs=tile(d),
        out_shape=jax.ShapeDtypeStruct(x.shape, F32),
        compiler_params=pltpu.CompilerParams(
            dimension_semantics=("arbitrary", "arbitrary"), vmem_limit_bytes=VMEM_LIMIT),
        name="mix_out",
    )(x, mod, g_mix, attn, sgu, w_in, w_attn_br, w_sgu_br, w_out)


def _conv_ffn_kernel(*refs, final_norm, n_cast):
    (x_ref, mod_ref, g_ref, wup_ref, taps_ref, wdown_ref, gfin_ref), refs = refs[:7], refs[7:]
    cast_src, refs = refs[:n_cast], refs[n_cast:]
    o_ref, refs = refs[0], refs[1:]
    cast_dst, refs = refs[:n_cast], refs[n_cast:]
    h_ref, gated_ref, tail_ref, wina_ref, winl_ref = refs

    i = pl.program_id(1)
    tm = x_ref.shape[0]
    d_ff = wdown_ref.shape[0]
    tf = FF_CHUNK
    x = x_ref[...]
    h_ref[...] = _norm_mod(x, g_ref[...], mod_ref[3:4, :], mod_ref[4:5, :]).astype(BF16)

    for src, dst in zip(cast_src, cast_dst):
        dst[...] = src[...].astype(BF16)

    def conv_chunk(cols, win_ref):
        up = _dot(h_ref[...], wup_ref[:, cols])
        win_ref[:SUBLANES, :] = jnp.where(i > 0, tail_ref[:, cols], 0.0)
        win_ref[SUBLANES:, :] = up
        tail_ref[:, cols] = up[tm - SUBLANES:, :]
        taps = taps_ref[:, cols]
        conv = taps[3:4, :] + taps[2:3, :] * up
        for back in (1, 2):
            conv = conv + taps[2 - back:3 - back, :] * win_ref[pl.ds(SUBLANES - back, tm), :]
        return conv

    for j in range(d_ff // tf):
        act = conv_chunk(slice(j * tf, (j + 1) * tf), wina_ref)
        lin = conv_chunk(slice(d_ff + j * tf, d_ff + (j + 1) * tf), winl_ref)
        gated_ref[:, j * tf:(j + 1) * tf] = (_silu(act) * lin).astype(BF16)

    out = x + mod_ref[5:6, :] * _dot(gated_ref[...], wdown_ref[...])
    if final_norm:
        out = _rms(out, gfin_ref[...])
    o_ref[...] = out


def _conv_ffn(layer, x, mod, g_ffn, w_up, taps, w_down, g_final, final_norm, next_weights):
    b, s, d = x.shape
    tm = TOKEN_TILE
    n_tiles = s // tm
    d_ff = w_down.shape[0]
    tile = pl.BlockSpec((None, tm, d), lambda bi, i: (bi, i, 0))
    cast_in, cast_out, cast_shapes = [], [], []
    for w in next_weights:
        _, rows, cols = w.shape
        share = rows // (b * n_tiles)
        assert share * b * n_tiles == rows and share % BF16_ROWS == 0
        cast_in.append(pl.BlockSpec(
            (None, share, cols), lambda bi, i: (layer + 1, bi * n_tiles + i, 0)))
        cast_out.append(pl.BlockSpec((share, cols), lambda bi, i: (bi * n_tiles + i, 0)))
        cast_shapes.append(jax.ShapeDtypeStruct((rows, cols), BF16))
    outs = pl.pallas_call(
        functools.partial(_conv_ffn_kernel, final_norm=final_norm, n_cast=len(next_weights)),
        grid=(b, n_tiles),
        in_specs=[
            tile,
            pl.BlockSpec((None, None, N_MOD, d), lambda bi, i: (layer, bi, 0, 0)),
            _layer_spec(g_ffn.shape, layer),
            _full_spec(w_up.shape),
            _layer_spec(taps.shape, layer),
            _full_spec(w_down.shape),
            _full_spec(g_final.shape),
        ] + cast_in,
        out_specs=[tile] + cast_out,
        out_shape=[jax.ShapeDtypeStruct(x.shape, F32)] + cast_shapes,
        scratch_shapes=[
            pltpu.VMEM((tm, d), BF16),
            pltpu.VMEM((tm, d_ff), BF16),
            pltpu.VMEM((SUBLANES, 2 * d_ff), F32),
            pltpu.VMEM((SUBLANES + tm, FF_CHUNK), F32),
            pltpu.VMEM((SUBLANES + tm, FF_CHUNK), F32),
        ],
        compiler_params=pltpu.CompilerParams(
            dimension_semantics=("arbitrary", "arbitrary"), vmem_limit_bytes=VMEM_LIMIT),
        name="conv_ffn",
    )(x, mod, g_ffn, w_up, taps, w_down, g_final, *next_weights)
    return outs[0], tuple(outs[1:])


def kernel(x, c, w_mod, b_mod, g_mix, w_in, g_sgu, w_sgu_s, b_sgu_s, w_attn_br, w_sgu_br, w_out,
           g_ffn, w_up, w_conv, b_conv, w_down, g_final):
    b, s, d = x.shape
    depth = w_mod.shape[0]
    d_ff = w_down.shape[1]
    assert s % TOKEN_TILE == 0 and TOKEN_TILE % MOBA_BLOCK == 0 and d_ff % FF_CHUNK == 0

    w_down_view = w_down.reshape(depth, 2 * d_ff, d // 2)
    weights_f32 = (w_in, w_attn_br, w_sgu_br, w_out, w_up, w_down_view)
    weights = tuple(w[0].astype(BF16) for w in weights_f32)
    b_sp = jnp.repeat(jnp.swapaxes(b_sgu_s, 1, 2), SGU_GROUP_DIM, axis=2)
    taps = jnp.concatenate(
        [w_conv, b_conv[:, None, :],
         jnp.zeros((depth, SUBLANES - CONV_WIDTH - 1, 2 * d_ff), F32)], axis=1)
    g_mix3, g_sgu3, g_ffn3 = (g.reshape(depth, 1, -1) for g in (g_mix, g_sgu, g_ffn))
    g_fin = g_final.reshape(1, d)

    mod = _modulation(c, w_mod, b_mod).reshape(depth, b, N_MOD, d)
    for l in range(depth):
        w_in_b, w_attn_b, w_sgu_b, w_out_b, w_up_b, w_down_b = weights
        last = l == depth - 1
        q, k, v, kmean, sgu = _mix_in(l, x, mod, g_mix3, w_in_b, g_sgu3, w_sgu_s, b_sp)
        attn = _moba(q, k, v, kmean.reshape(b, s // MOBA_BLOCK, ATTN_WIDTH))
        x = _mix_out(l, x, mod, g_mix3, attn, sgu, w_in_b, w_attn_b, w_sgu_b, w_out_b)
        x, weights = _conv_ffn(l, x, mod, g_ffn3, w_up_b, taps, w_down_b.reshape(d_ff, d), g_fin,
                               final_norm=last, next_weights=() if last else weights_f32)
    return x
```

```python
import functools
import math

import jax
import jax.numpy as jnp
from jax import lax
from jax.experimental import pallas as pl
from jax.experimental.pallas import tpu as pltpu

F32 = jnp.float32
BF16 = jnp.bfloat16

N_HEADS = 8
HEAD_DIM = 64
ATTN_WIDTH = N_HEADS * HEAD_DIM
MOBA_BLOCK = 256
MOBA_TOPK = 3
SGU_GROUPS = 8
SGU_GROUP_DIM = 64
SGU_WIDTH = SGU_GROUPS * SGU_GROUP_DIM
SGU_CHUNK = 128
CONV_WIDTH = 3
N_MOD = 6
EPS = 1e-6
QKV_WIDTH = 3 * ATTN_WIDTH
UV_WIDTH = 2 * SGU_WIDTH

LANES = 128
SUBLANES = 8
BF16_ROWS = 16
HEADS_PER_STEP = LANES // HEAD_DIM
VMEM_LIMIT = 56 * 1024 * 1024

TOKEN_TILE = 512
FF_CHUNK = 256
MOD_COLS = 2048
SCORE_LOOKAHEAD = 3
Q_SCALE = HEAD_DIM ** -0.5 * math.log2(math.e)


def _dot(a, b):
    return jnp.dot(a, b, preferred_element_type=F32)


def _dot_nt(a, b):
    return lax.dot_general(a, b, (((1,), (1,)), ((), ())), preferred_element_type=F32)


def _sigmoid(x):
    return 0.5 + 0.5 * jnp.tanh(0.5 * x)


def _silu(x):
    half = 0.5 * x
    return half + half * jnp.tanh(half)


def _gelu_tanh(x):
    c = 0.7978845608028654
    return 0.5 * x * (1.0 + jnp.tanh(c * (x + 0.044715 * (x * x * x))))


def _rms(x, g):
    return x * lax.rsqrt(jnp.mean(x * x, axis=-1, keepdims=True) + EPS) * g


def _norm_mod(x, g, shift, scale):
    return _rms(x, g * (1.0 + scale)) + shift


def _layer_spec(shape, layer):
    zeros = (0,) * (len(shape) - 1)
    return pl.BlockSpec((None,) + tuple(shape[1:]), lambda *_: (layer,) + zeros)


def _full_spec(shape):
    zeros = (0,) * len(shape)
    return pl.BlockSpec(tuple(shape), lambda *_: zeros)


def _mod_kernel(c_ref, w_ref, b_ref, o_ref):
    c = c_ref[...]
    c_act = _silu(c).astype(BF16)
    o_ref[...] = _dot(c_act, w_ref[...].astype(BF16)) + b_ref[...]


def _modulation(c, w_mod, b_mod):
    depth, d, n = w_mod.shape
    b = c.shape[0]
    return pl.pallas_call(
        _mod_kernel,
        grid=(depth, n // MOD_COLS),
        in_specs=[
            pl.BlockSpec((b, d), lambda l, j: (0, 0)),
            pl.BlockSpec((None, d, MOD_COLS), lambda l, j: (l, 0, j)),
            pl.BlockSpec((None, 1, MOD_COLS), lambda l, j: (l, 0, j)),
        ],
        out_specs=pl.BlockSpec((None, b, MOD_COLS), lambda l, j: (l, 0, j)),
        out_shape=jax.ShapeDtypeStruct((depth, b, n), F32),
        compiler_params=pltpu.CompilerParams(
            dimension_semantics=("arbitrary", "arbitrary"), vmem_limit_bytes=VMEM_LIMIT),
        name="modulation",
    )(c, w_mod, b_mod.reshape(depth, 1, n))


def _mix_in_kernel(x_ref, mod_ref, g_ref, win_ref, gsgu_ref, wsgu_ref, bsp_ref,
                   q_ref, k_ref, v_ref, kmean_ref, sgu_ref):
    tm = x_ref.shape[0]
    h = _norm_mod(x_ref[...], g_ref[...], mod_ref[0:1, :], mod_ref[1:2, :]).astype(BF16)

    uv = _dot(h, win_ref[:, QKV_WIDTH:QKV_WIDTH + UV_WIDTH])
    qk = _dot(h, win_ref[:, :2 * ATTN_WIDTH])
    q_ref[...] = (qk[:, :ATTN_WIDTH] * Q_SCALE).astype(BF16)
    k = qk[:, ATTN_WIDTH:]
    k_ref[...] = k.astype(BF16)
    for c in range(tm // MOBA_BLOCK):
        kmean_ref[c] = jnp.mean(k[c * MOBA_BLOCK:(c + 1) * MOBA_BLOCK], axis=0, keepdims=True)

    gu = _gelu_tanh(uv[:, :SGU_WIDTH])
    vn = _rms(_gelu_tanh(uv[:, SGU_WIDTH:]), gsgu_ref[...]).astype(BF16)

    lane = lax.broadcasted_iota(jnp.int32, (SGU_CHUNK, LANES), 1)
    first = lane < SGU_GROUP_DIM
    wrow = lax.broadcasted_iota(jnp.int32, (SGU_CHUNK, 2 * SGU_CHUNK), 0)
    wcol = lax.broadcasted_iota(jnp.int32, (SGU_CHUNK, 2 * SGU_CHUNK), 1)
    causal = jnp.where(wcol >= SGU_CHUNK, wcol - SGU_CHUNK, wcol) <= wrow
    zero = jnp.zeros((), BF16)
    for gp in range(SGU_WIDTH // LANES):
        cols = slice(gp * LANES, (gp + 1) * LANES)
        w = jnp.concatenate([wsgu_ref[2 * gp], wsgu_ref[2 * gp + 1]], axis=1)
        w = jnp.where(causal, w, 0.0).astype(BF16)
        bias = bsp_ref[:, cols]
        for c in range(tm // SGU_CHUNK):
            rows = slice(c * SGU_CHUNK, (c + 1) * SGU_CHUNK)
            vb = vn[rows, cols]
            rhs = jnp.concatenate([jnp.where(first, vb, zero), jnp.where(first, zero, vb)], axis=0)
            mixed = _dot(w, rhs) + bias
            sgu_ref[rows, cols] = (gu[rows, cols] * mixed).astype(BF16)

    v_ref[...] = _dot(h, win_ref[:, 2 * ATTN_WIDTH:QKV_WIDTH]).astype(BF16)


def _mix_in(layer, x, mod, g_mix, w_in, g_sgu, w_sgu_s, b_sp):
    b, s, d = x.shape
    tm = TOKEN_TILE
    blocks_per_tile = tm // MOBA_BLOCK
    act = lambda width: jax.ShapeDtypeStruct((b, s, width), BF16)
    act_spec = lambda width: pl.BlockSpec((None, tm, width), lambda bi, i: (bi, i, 0))
    return pl.pallas_call(
        _mix_in_kernel,
        grid=(b, s // tm),
        in_specs=[
            pl.BlockSpec((None, tm, d), lambda bi, i: (bi, i, 0)),
            pl.BlockSpec((None, None, N_MOD, d), lambda bi, i: (layer, bi, 0, 0)),
            _layer_spec(g_mix.shape, layer),
            pl.BlockSpec((d, QKV_WIDTH + UV_WIDTH), lambda bi, i: (0, 0)),
            _layer_spec(g_sgu.shape, layer),
            _layer_spec(w_sgu_s.shape, layer),
            _layer_spec(b_sp.shape, layer),
        ],
        out_specs=[
            act_spec(ATTN_WIDTH), act_spec(ATTN_WIDTH), act_spec(ATTN_WIDTH),
            pl.BlockSpec((None, blocks_per_tile, 1, ATTN_WIDTH), lambda bi, i: (bi, i, 0, 0)),
            act_spec(SGU_WIDTH),
        ],
        out_shape=[
            act(ATTN_WIDTH), act(ATTN_WIDTH), act(ATTN_WIDTH),
            jax.ShapeDtypeStruct((b, s // MOBA_BLOCK, 1, ATTN_WIDTH), F32),
            act(SGU_WIDTH),
        ],
        compiler_params=pltpu.CompilerParams(
            dimension_semantics=("arbitrary", "arbitrary"), vmem_limit_bytes=VMEM_LIMIT),
        name="mix_in",
    )(x, mod, g_mix, w_in, g_sgu, w_sgu_s, b_sp)


def _moba_kernel(q_ref, k_ref, v_ref, km_ref, o_ref, vt_ref, ot_ref):
    nb = km_ref.shape[0]
    s_len = q_ref.shape[0]
    neg_inf = jnp.float32(-jnp.inf)
    blocks = [slice(j * MOBA_BLOCK, (j + 1) * MOBA_BLOCK) for j in range(nb)]

    q = q_ref[...]
    lane = lax.broadcasted_iota(jnp.int32, (s_len, LANES), 1)
    qh = []
    for h in range(HEADS_PER_STEP):
        in_head = (lane >= h * HEAD_DIM) & (lane < (h + 1) * HEAD_DIM)
        qh.append(jnp.where(in_head, q, jnp.zeros((), BF16)))

    def scores(h, i):
        return _dot_nt(k_ref[:(i + 1) * MOBA_BLOCK, :], qh[h][blocks[i]])

    order = [(h, i) for i in range(nb) for h in range(HEADS_PER_STEP)]
    pending = [scores(*order[n]) for n in range(SCORE_LOOKAHEAD)]

    for j in range(nb):
        vt = v_ref[blocks[j], :].astype(F32).T.astype(BF16)
        for h in range(HEADS_PER_STEP):
            vt_ref[h, :HEAD_DIM, blocks[j]] = vt[h * HEAD_DIM:(h + 1) * HEAD_DIM]
    for h in range(HEADS_PER_STEP):
        vt_ref[h, HEAD_DIM:, :] = jnp.ones((BF16_ROWS, s_len), BF16)

    km = km_ref[...]
    km_hi = km.astype(BF16)
    km_lo = (km - km_hi.astype(F32)).astype(BF16)
    blk = lax.broadcasted_iota(jnp.int32, (nb, s_len), 0)
    q_blk = lax.broadcasted_iota(jnp.int32, (nb, s_len), 1) // MOBA_BLOCK
    kpos = lax.broadcasted_iota(jnp.int32, (MOBA_BLOCK, MOBA_BLOCK), 0)
    qpos = lax.broadcasted_iota(jnp.int32, (MOBA_BLOCK, MOBA_BLOCK), 1)
    causal = kpos <= qpos

    bias = []
    for h in range(HEADS_PER_STEP):
        route = _dot_nt(km_hi, qh[h]) + _dot_nt(km_lo, qh[h])
        rank = jnp.zeros((nb, s_len), jnp.int32)
        for jp in range(nb):
            rj = route[jp:jp + 1, :]
            beats = (rj > route) | ((rj == route) & (blk > jp))
            rank = rank + jnp.where(beats & (q_blk > jp), 1, 0)
        bias.append(jnp.where((blk < q_blk) & (rank < MOBA_TOPK), 0.0, neg_inf))

    for n, (h, i) in enumerate(order):
        s = pending.pop(0)
        if n + SCORE_LOOKAHEAD < len(order):
            pending.append(scores(*order[n + SCORE_LOOKAHEAD]))
        parts = [s[blocks[j]] + bias[h][j:j + 1, blocks[i]] for j in range(i)]
        parts.append(jnp.where(causal, s[blocks[i]], neg_inf))
        s = jnp.concatenate(parts, axis=0)
        m = jnp.max(s, axis=0, keepdims=True)
        p = jnp.exp2(s - m).astype(BF16)
        pv = _dot(vt_ref[h, :, :(i + 1) * MOBA_BLOCK], p)
        ot_ref[h * HEAD_DIM:(h + 1) * HEAD_DIM, blocks[i]] = (
            pv[:HEAD_DIM] / pv[HEAD_DIM:HEAD_DIM + 1])
        if h == HEADS_PER_STEP - 1:
            o_ref[blocks[i], :] = ot_ref[:, blocks[i]].T.astype(o_ref.dtype)


def _moba(q, k, v, kmean):
    b, s, width = q.shape
    nb = s // MOBA_BLOCK
    slab = pl.BlockSpec((None, s, LANES), lambda bi, p: (bi, 0, p))
    return pl.pallas_call(
        _moba_kernel,
        grid=(b, width // LANES),
        in_specs=[slab, slab, slab, pl.BlockSpec((None, nb, LANES), lambda bi, p: (bi, 0, p))],
        out_specs=slab,
        out_shape=jax.ShapeDtypeStruct((b, s, width), BF16),
        scratch_shapes=[
            pltpu.VMEM((HEADS_PER_STEP, HEAD_DIM + BF16_ROWS, s), BF16),
            pltpu.VMEM((LANES, s), F32),
        ],
        compiler_params=pltpu.CompilerParams(
            dimension_semantics=("arbitrary", "arbitrary"), vmem_limit_bytes=VMEM_LIMIT),
        name="moba",
    )(q, k, v, kmean)


def _mix_out_kernel(x_ref, mod_ref, g_ref, attn_ref, sgu_ref, win_ref, wa_ref, ws_ref, wo_ref,
                    o_ref):
    d = x_ref.shape[1]
    gate_cols = QKV_WIDTH + UV_WIDTH
    x = x_ref[...]
    h = _norm_mod(x, g_ref[...], mod_ref[0:1, :], mod_ref[1:2, :]).astype(BF16)
    gates = _dot(h, win_ref[:, gate_cols:])
    ya = _dot(attn_ref[...], wa_ref[...])
    ys = _dot(sgu_ref[...], ws_ref[...])
    merged = _sigmoid(gates[:, :d]) * ya + _sigmoid(gates[:, d:]) * ys
    o_ref[...] = x + mod_ref[2:3, :] * _dot(merged.astype(BF16), wo_ref[...])


def _mix_out(layer, x, mod, g_mix, attn, sgu, w_in, w_attn_br, w_sgu_br, w_out):
    b, s, d = x.shape
    tm = TOKEN_TILE
    tile = lambda width: pl.BlockSpec((None, tm, width), lambda bi, i: (bi, i, 0))
    return pl.pallas_call(
        _mix_out_kernel,
        grid=(b, s // tm),
        in_specs=[
            tile(d),
            pl.BlockSpec((None, None, N_MOD, d), lambda bi, i: (layer, bi, 0, 0)),
            _layer_spec(g_mix.shape, layer),
            tile(attn.shape[2]),
            tile(sgu.shape[2]),
            _full_spec(w_in.shape),
            _full_spec(w_attn_br.shape),
            _full_spec(w_sgu_br.shape),
            _full_spec(w_out.shape),
        ],
        out_specs=tile(d),
        out_shape=jax.ShapeDtypeStruct(x.shape, F32),
        compiler_params=pltpu.CompilerParams(
            dimension_semantics=("arbitrary", "arbitrary"), vmem_limit_bytes=VMEM_LIMIT),
        name="mix_out",
    )(x, mod, g_mix, attn, sgu, w_in, w_attn_br, w_sgu_br, w_out)


def _conv_ffn_kernel(*refs, final_norm, n_cast):
    (x_ref, mod_ref, g_ref, wup_ref, taps_ref, wdown_ref, gfin_ref), refs = refs[:7], refs[7:]
    cast_src, refs = refs[:n_cast], refs[n_cast:]
    o_ref, refs = refs[0], refs[1:]
    cast_dst, refs = refs[:n_cast], refs[n_cast:]
    h_ref, gated_ref, tail_ref, wina_ref, winl_ref = refs

    i = pl.program_id(1)
    tm = x_ref.shape[0]
    d_ff = wdown_ref.shape[0]
    tf = FF_CHUNK
    x = x_ref[...]
    h_ref[...] = _norm_mod(x, g_ref[...], mod_ref[3:4, :], mod_ref[4:5, :]).astype(BF16)

    for src, dst in zip(cast_src, cast_dst):
        dst[...] = src[...].astype(BF16)

    def conv_chunk(cols, win_ref):
        up = _dot(h_ref[...], wup_ref[:, cols])
        win_ref[:SUBLANES, :] = jnp.where(i > 0, tail_ref[:, cols], 0.0)
        win_ref[SUBLANES:, :] = up
        tail_ref[:, cols] = up[tm - SUBLANES:, :]
        taps = taps_ref[:, cols]
        conv = taps[3:4, :] + taps[2:3, :] * up
        for back in (1, 2):
            conv = conv + taps[2 - back:3 - back, :] * win_ref[pl.ds(SUBLANES - back, tm), :]
        return conv

    for j in range(d_ff // tf):
        act = conv_chunk(slice(j * tf, (j + 1) * tf), wina_ref)
        lin = conv_chunk(slice(d_ff + j * tf, d_ff + (j + 1) * tf), winl_ref)
        gated_ref[:, j * tf:(j + 1) * tf] = (_silu(act) * lin).astype(BF16)

    out = x + mod_ref[5:6, :] * _dot(gated_ref[...], wdown_ref[...])
    if final_norm:
        out = _rms(out, gfin_ref[...])
    o_ref[...] = out


def _conv_ffn(layer, x, mod, g_ffn, w_up, taps, w_down, g_final, final_norm, next_weights):
    b, s, d = x.shape
    tm = TOKEN_TILE
    n_tiles = s // tm
    d_ff = w_down.shape[0]
    tile = pl.BlockSpec((None, tm, d), lambda bi, i: (bi, i, 0))
    cast_in, cast_out, cast_shapes = [], [], []
    for w in next_weights:
        _, rows, cols = w.shape
        least = -(-rows // (b * n_tiles * BF16_ROWS)) * BF16_ROWS
        share = next(r for r in range(least, rows + 1, BF16_ROWS) if rows % r == 0)

        def block(bi, i, last=rows // share - 1):
            return jnp.minimum(bi * n_tiles + i, last)

        cast_in.append(pl.BlockSpec(
            (None, share, cols), lambda bi, i, block=block: (layer + 1, block(bi, i), 0)))
        cast_out.append(pl.BlockSpec(
            (share, cols), lambda bi, i, block=block: (block(bi, i), 0)))
        cast_shapes.append(jax.ShapeDtypeStruct((rows, cols), BF16))
    outs = pl.pallas_call(
        functools.partial(_conv_ffn_kernel, final_norm=final_norm, n_cast=len(next_weights)),
        grid=(b, n_tiles),
        in_specs=[
            tile,
            pl.BlockSpec((None, None, N_MOD, d), lambda bi, i: (layer, bi, 0, 0)),
            _layer_spec(g_ffn.shape, layer),
            _full_spec(w_up.shape),
            _layer_spec(taps.shape, layer),
            _full_spec(w_down.shape),
            _full_spec(g_final.shape),
        ] + cast_in,
        out_specs=[tile] + cast_out,
        out_shape=[jax.ShapeDtypeStruct(x.shape, F32)] + cast_shapes,
        scratch_shapes=[
            pltpu.VMEM((tm, d), BF16),
            pltpu.VMEM((tm, d_ff), BF16),
            pltpu.VMEM((SUBLANES, 2 * d_ff), F32),
            pltpu.VMEM((SUBLANES + tm, FF_CHUNK), F32),
            pltpu.VMEM((SUBLANES + tm, FF_CHUNK), F32),
        ],
        compiler_params=pltpu.CompilerParams(
            dimension_semantics=("arbitrary", "arbitrary"), vmem_limit_bytes=VMEM_LIMIT),
        name="conv_ffn",
    )(x, mod, g_ffn, w_up, taps, w_down, g_final, *next_weights)
    return outs[0], tuple(outs[1:])


def kernel(x, c, w_mod, b_mod, g_mix, w_in, g_sgu, w_sgu_s, b_sgu_s, w_attn_br, w_sgu_br, w_out,
           g_ffn, w_up, w_conv, b_conv, w_down, g_final):
    b, s, d = x.shape
    depth = w_mod.shape[0]
    d_ff = w_down.shape[1]
    assert s % TOKEN_TILE == 0 and TOKEN_TILE % MOBA_BLOCK == 0 and d_ff % FF_CHUNK == 0

    weights_f32 = (w_in, w_attn_br, w_sgu_br, w_out, w_up, w_down)
    weights = tuple(w[0].astype(BF16) for w in weights_f32)
    b_sp = jnp.repeat(jnp.swapaxes(b_sgu_s, 1, 2), SGU_GROUP_DIM, axis=2)
    taps = jnp.concatenate(
        [w_conv, b_conv[:, None, :],
         jnp.zeros((depth, SUBLANES - CONV_WIDTH - 1, 2 * d_ff), F32)], axis=1)
    g_mix3, g_sgu3, g_ffn3 = (g.reshape(depth, 1, -1) for g in (g_mix, g_sgu, g_ffn))
    g_fin = g_final.reshape(1, d)

    mod = _modulation(c, w_mod, b_mod).reshape(depth, b, N_MOD, d)
    for l in range(depth):
        w_in_b, w_attn_b, w_sgu_b, w_out_b, w_up_b, w_down_b = weights
        last = l == depth - 1
        q, k, v, kmean, sgu = _mix_in(l, x, mod, g_mix3, w_in_b, g_sgu3, w_sgu_s, b_sp)
        attn = _moba(q, k, v, kmean.reshape(b, s // MOBA_BLOCK, ATTN_WIDTH))
        x = _mix_out(l, x, mod, g_mix3, attn, sgu, w_in_b, w_attn_b, w_sgu_b, w_out_b)
        x, weights = _conv_ffn(l, x, mod, g_ffn3, w_up_b, taps, w_down_b, g_fin,
                               final_norm=last, next_weights=() if last else weights_f32)
    return x
```

```python
import functools
import math

import jax
import jax.numpy as jnp
from jax import lax
from jax.experimental import pallas as pl
from jax.experimental.pallas import tpu as pltpu

F32 = jnp.float32
BF16 = jnp.bfloat16

N_HEADS = 8
HEAD_DIM = 64
ATTN_WIDTH = N_HEADS * HEAD_DIM
MOBA_BLOCK = 256
MOBA_TOPK = 3
SGU_GROUPS = 8
SGU_GROUP_DIM = 64
SGU_WIDTH = SGU_GROUPS * SGU_GROUP_DIM
SGU_CHUNK = 128
CONV_WIDTH = 3
N_MOD = 6
EPS = 1e-6
QKV_WIDTH = 3 * ATTN_WIDTH
UV_WIDTH = 2 * SGU_WIDTH

LANES = 128
SUBLANES = 8
BF16_ROWS = 16
HEADS_PER_STEP = LANES // HEAD_DIM
VMEM_LIMIT = 56 * 1024 * 1024

TOKEN_TILE = 1024
FF_CHUNK = 256
MOD_COLS = 2048
SCORE_LOOKAHEAD = 3
Q_SCALE = HEAD_DIM ** -0.5 * math.log2(math.e)


def _dot(a, b):
    return jnp.dot(a, b, preferred_element_type=F32)


def _dot_nt(a, b):
    return lax.dot_general(a, b, (((1,), (1,)), ((), ())), preferred_element_type=F32)


def _sigmoid(x):
    return 0.5 + 0.5 * jnp.tanh(0.5 * x)


def _silu(x):
    half = 0.5 * x
    return half + half * jnp.tanh(half)


def _gelu_tanh(x):
    c = 0.7978845608028654
    return 0.5 * x * (1.0 + jnp.tanh(c * (x + 0.044715 * (x * x * x))))


def _rms(x, g):
    return x * lax.rsqrt(jnp.mean(x * x, axis=-1, keepdims=True) + EPS) * g


def _norm_mod(x, g, shift, scale):
    return _rms(x, g * (1.0 + scale)) + shift


def _layer_spec(shape, layer):
    zeros = (0,) * (len(shape) - 1)
    return pl.BlockSpec((None,) + tuple(shape[1:]), lambda *_: (layer,) + zeros)


def _full_spec(shape):
    zeros = (0,) * len(shape)
    return pl.BlockSpec(tuple(shape), lambda *_: zeros)


def _mod_kernel(c_ref, w_ref, b_ref, o_ref):
    c = c_ref[...]
    c_act = _silu(c).astype(BF16)
    o_ref[...] = _dot(c_act, w_ref[...].astype(BF16)) + b_ref[...]


def _modulation(c, w_mod, b_mod):
    depth, d, n = w_mod.shape
    b = c.shape[0]
    return pl.pallas_call(
        _mod_kernel,
        grid=(depth, n // MOD_COLS),
        in_specs=[
            pl.BlockSpec((b, d), lambda l, j: (0, 0)),
            pl.BlockSpec((None, d, MOD_COLS), lambda l, j: (l, 0, j)),
            pl.BlockSpec((None, 1, MOD_COLS), lambda l, j: (l, 0, j)),
        ],
        out_specs=pl.BlockSpec((None, b, MOD_COLS), lambda l, j: (l, 0, j)),
        out_shape=jax.ShapeDtypeStruct((depth, b, n), F32),
        compiler_params=pltpu.CompilerParams(
            dimension_semantics=("arbitrary", "arbitrary"), vmem_limit_bytes=VMEM_LIMIT),
        name="modulation",
    )(c, w_mod, b_mod.reshape(depth, 1, n))


def _mix_in_kernel(x_ref, mod_ref, g_ref, win_ref, gsgu_ref, wsgu_ref, bsp_ref,
                   q_ref, k_ref, v_ref, kmean_ref, sgu_ref):
    tm = x_ref.shape[0]
    h = _norm_mod(x_ref[...], g_ref[...], mod_ref[0:1, :], mod_ref[1:2, :]).astype(BF16)

    uv = _dot(h, win_ref[:, QKV_WIDTH:QKV_WIDTH + UV_WIDTH])
    qk = _dot(h, win_ref[:, :2 * ATTN_WIDTH])
    q_ref[...] = (qk[:, :ATTN_WIDTH] * Q_SCALE).astype(BF16)
    k = qk[:, ATTN_WIDTH:]
    k_ref[...] = k.astype(BF16)
    for c in range(tm // MOBA_BLOCK):
        kmean_ref[c] = jnp.mean(k[c * MOBA_BLOCK:(c + 1) * MOBA_BLOCK], axis=0, keepdims=True)

    gu = _gelu_tanh(uv[:, :SGU_WIDTH])
    vn = _rms(_gelu_tanh(uv[:, SGU_WIDTH:]), gsgu_ref[...]).astype(BF16)

    lane = lax.broadcasted_iota(jnp.int32, (SGU_CHUNK, LANES), 1)
    first = lane < SGU_GROUP_DIM
    wrow = lax.broadcasted_iota(jnp.int32, (SGU_CHUNK, 2 * SGU_CHUNK), 0)
    wcol = lax.broadcasted_iota(jnp.int32, (SGU_CHUNK, 2 * SGU_CHUNK), 1)
    causal = jnp.where(wcol >= SGU_CHUNK, wcol - SGU_CHUNK, wcol) <= wrow
    zero = jnp.zeros((), BF16)
    for gp in range(SGU_WIDTH // LANES):
        cols = slice(gp * LANES, (gp + 1) * LANES)
        w = jnp.concatenate([wsgu_ref[2 * gp], wsgu_ref[2 * gp + 1]], axis=1)
        w = jnp.where(causal, w, 0.0).astype(BF16)
        bias = bsp_ref[:, cols]
        for c in range(tm // SGU_CHUNK):
            rows = slice(c * SGU_CHUNK, (c + 1) * SGU_CHUNK)
            vb = vn[rows, cols]
            rhs = jnp.concatenate([jnp.where(first, vb, zero), jnp.where(first, zero, vb)], axis=0)
            mixed = _dot(w, rhs) + bias
            sgu_ref[rows, cols] = (gu[rows, cols] * mixed).astype(BF16)

    v_ref[...] = _dot(h, win_ref[:, 2 * ATTN_WIDTH:QKV_WIDTH]).astype(BF16)


def _mix_in(layer, x, mod, g_mix, w_in, g_sgu, w_sgu_s, b_sp):
    b, s, d = x.shape
    tm = TOKEN_TILE
    blocks_per_tile = tm // MOBA_BLOCK
    act = lambda width: jax.ShapeDtypeStruct((b, s, width), BF16)
    act_spec = lambda width: pl.BlockSpec((None, tm, width), lambda bi, i: (bi, i, 0))
    return pl.pallas_call(
        _mix_in_kernel,
        grid=(b, s // tm),
        in_specs=[
            pl.BlockSpec((None, tm, d), lambda bi, i: (bi, i, 0)),
            pl.BlockSpec((None, None, N_MOD, d), lambda bi, i: (layer, bi, 0, 0)),
            _layer_spec(g_mix.shape, layer),
            pl.BlockSpec((d, QKV_WIDTH + UV_WIDTH), lambda bi, i: (0, 0)),
            _layer_spec(g_sgu.shape, layer),
            _layer_spec(w_sgu_s.shape, layer),
            _layer_spec(b_sp.shape, layer),
        ],
        out_specs=[
            act_spec(ATTN_WIDTH), act_spec(ATTN_WIDTH), act_spec(ATTN_WIDTH),
            pl.BlockSpec((None, blocks_per_tile, 1, ATTN_WIDTH), lambda bi, i: (bi, i, 0, 0)),
            act_spec(SGU_WIDTH),
        ],
        out_shape=[
            act(ATTN_WIDTH), act(ATTN_WIDTH), act(ATTN_WIDTH),
            jax.ShapeDtypeStruct((b, s // MOBA_BLOCK, 1, ATTN_WIDTH), F32),
            act(SGU_WIDTH),
        ],
        compiler_params=pltpu.CompilerParams(
            dimension_semantics=("arbitrary", "arbitrary"), vmem_limit_bytes=VMEM_LIMIT),
        name="mix_in",
    )(x, mod, g_mix, w_in, g_sgu, w_sgu_s, b_sp)


def _moba_kernel(q_ref, k_ref, v_ref, km_ref, o_ref, vt_ref, ot_ref):
    nb = km_ref.shape[0]
    s_len = q_ref.shape[0]
    neg_inf = jnp.float32(-jnp.inf)
    blocks = [slice(j * MOBA_BLOCK, (j + 1) * MOBA_BLOCK) for j in range(nb)]

    q = q_ref[...]
    lane = lax.broadcasted_iota(jnp.int32, (s_len, LANES), 1)
    qh = []
    for h in range(HEADS_PER_STEP):
        in_head = (lane >= h * HEAD_DIM) & (lane < (h + 1) * HEAD_DIM)
        qh.append(jnp.where(in_head, q, jnp.zeros((), BF16)))

    def scores(h, i):
        return _dot_nt(k_ref[:(i + 1) * MOBA_BLOCK, :], qh[h][blocks[i]])

    order = [(h, i) for i in range(nb) for h in range(HEADS_PER_STEP)]
    pending = [scores(*order[n]) for n in range(SCORE_LOOKAHEAD)]

    for j in range(nb):
        vt = v_ref[blocks[j], :].astype(F32).T.astype(BF16)
        for h in range(HEADS_PER_STEP):
            vt_ref[h, :HEAD_DIM, blocks[j]] = vt[h * HEAD_DIM:(h + 1) * HEAD_DIM]
    for h in range(HEADS_PER_STEP):
        vt_ref[h, HEAD_DIM:, :] = jnp.ones((BF16_ROWS, s_len), BF16)

    km = km_ref[...]
    km_hi = km.astype(BF16)
    km_lo = (km - km_hi.astype(F32)).astype(BF16)
    blk = lax.broadcasted_iota(jnp.int32, (nb, s_len), 0)
    q_blk = lax.broadcasted_iota(jnp.int32, (nb, s_len), 1) // MOBA_BLOCK
    kpos = lax.broadcasted_iota(jnp.int32, (MOBA_BLOCK, MOBA_BLOCK), 0)
    qpos = lax.broadcasted_iota(jnp.int32, (MOBA_BLOCK, MOBA_BLOCK), 1)
    causal = kpos <= qpos

    bias = []
    for h in range(HEADS_PER_STEP):
        route = _dot_nt(km_hi, qh[h]) + _dot_nt(km_lo, qh[h])
        rank = jnp.zeros((nb, s_len), jnp.int32)
        for jp in range(nb):
            rj = route[jp:jp + 1, :]
            beats = (rj > route) | ((rj == route) & (blk > jp))
            rank = rank + jnp.where(beats & (q_blk > jp), 1, 0)
        bias.append(jnp.where((blk < q_blk) & (rank < MOBA_TOPK), 0.0, neg_inf))

    for n, (h, i) in enumerate(order):
        s = pending.pop(0)
        if n + SCORE_LOOKAHEAD < len(order):
            pending.append(scores(*order[n + SCORE_LOOKAHEAD]))
        parts = [s[blocks[j]] + bias[h][j:j + 1, blocks[i]] for j in range(i)]
        parts.append(jnp.where(causal, s[blocks[i]], neg_inf))
        s = jnp.concatenate(parts, axis=0)
        m = jnp.max(s, axis=0, keepdims=True)
        p = jnp.exp2(s - m).astype(BF16)
        pv = _dot(vt_ref[h, :, :(i + 1) * MOBA_BLOCK], p)
        ot_ref[h * HEAD_DIM:(h + 1) * HEAD_DIM, blocks[i]] = (
            pv[:HEAD_DIM] / pv[HEAD_DIM:HEAD_DIM + 1])
        if h == HEADS_PER_STEP - 1:
            o_ref[blocks[i], :] = ot_ref[:, blocks[i]].T.astype(o_ref.dtype)


def _moba(q, k, v, kmean):
    b, s, width = q.shape
    nb = s // MOBA_BLOCK
    slab = pl.BlockSpec((None, s, LANES), lambda bi, p: (bi, 0, p))
    return pl.pallas_call(
        _moba_kernel,
        grid=(b, width // LANES),
        in_specs=[slab, slab, slab, pl.BlockSpec((None, nb, LANES), lambda bi, p: (bi, 0, p))],
        out_specs=slab,
        out_shape=jax.ShapeDtypeStruct((b, s, width), BF16),
        scratch_shapes=[
            pltpu.VMEM((HEADS_PER_STEP, HEAD_DIM + BF16_ROWS, s), BF16),
            pltpu.VMEM((LANES, s), F32),
        ],
        compiler_params=pltpu.CompilerParams(
            dimension_semantics=("arbitrary", "arbitrary"), vmem_limit_bytes=VMEM_LIMIT),
        name="moba",
    )(q, k, v, kmean)


def _mix_out_kernel(x_ref, mod_ref, g_ref, attn_ref, sgu_ref, win_ref, wa_ref, ws_ref, wo_ref,
                    o_ref):
    d = x_ref.shape[1]
    gate_cols = QKV_WIDTH + UV_WIDTH
    x = x_ref[...]
    h = _norm_mod(x, g_ref[...], mod_ref[0:1, :], mod_ref[1:2, :]).astype(BF16)
    gates = _dot(h, win_ref[:, gate_cols:])
    ya = _dot(attn_ref[...], wa_ref[...])
    ys = _dot(sgu_ref[...], ws_ref[...])
    merged = _sigmoid(gates[:, :d]) * ya + _sigmoid(gates[:, d:]) * ys
    o_ref[...] = x + mod_ref[2:3, :] * _dot(merged.astype(BF16), wo_ref[...])


def _mix_out(layer, x, mod, g_mix, attn, sgu, w_in, w_attn_br, w_sgu_br, w_out):
    b, s, d = x.shape
    tm = TOKEN_TILE
    tile = lambda width: pl.BlockSpec((None, tm, width), lambda bi, i: (bi, i, 0))
    return pl.pallas_call(
        _mix_out_kernel,
        grid=(b, s // tm),
        in_specs=[
            tile(d),
            pl.BlockSpec((None, None, N_MOD, d), lambda bi, i: (layer, bi, 0, 0)),
            _layer_spec(g_mix.shape, layer),
            tile(attn.shape[2]),
            tile(sgu.shape[2]),
            _full_spec(w_in.shape),
            _full_spec(w_attn_br.shape),
            _full_spec(w_sgu_br.shape),
            _full_spec(w_out.shape),
        ],
        out_specs=tile(d),
        out_shape=jax.ShapeDtypeStruct(x.shape, F32),
        compiler_params=pltpu.CompilerParams(
            dimension_semantics=("arbitrary", "arbitrary"), vmem_limit_bytes=VMEM_LIMIT),
        name="mix_out",
    )(x, mod, g_mix, attn, sgu, w_in, w_attn_br, w_sgu_br, w_out)


def _conv_ffn_kernel(*refs, final_norm, n_cast):
    (x_ref, mod_ref, g_ref, wup_ref, taps_ref, wdown_ref, gfin_ref), refs = refs[:7], refs[7:]
    cast_src, refs = refs[:n_cast], refs[n_cast:]
    o_ref, refs = refs[0], refs[1:]
    cast_dst, refs = refs[:n_cast], refs[n_cast:]
    h_ref, gated_ref, tail_ref, wina_ref, winl_ref = refs

    i = pl.program_id(1)
    tm = x_ref.shape[0]
    d_ff = wdown_ref.shape[0]
    tf = FF_CHUNK
    x = x_ref[...]
    h_ref[...] = _norm_mod(x, g_ref[...], mod_ref[3:4, :], mod_ref[4:5, :]).astype(BF16)

    for src, dst in zip(cast_src, cast_dst):
        dst[...] = src[...].astype(BF16)

    def conv_chunk(cols, win_ref):
        up = _dot(h_ref[...], wup_ref[:, cols])
        win_ref[:SUBLANES, :] = jnp.where(i > 0, tail_ref[:, cols], 0.0)
        win_ref[SUBLANES:, :] = up
        tail_ref[:, cols] = up[tm - SUBLANES:, :]
        taps = taps_ref[:, cols]
        conv = taps[3:4, :] + taps[2:3, :] * up
        for back in (1, 2):
            conv = conv + taps[2 - back:3 - back, :] * win_ref[pl.ds(SUBLANES - back, tm), :]
        return conv

    for j in range(d_ff // tf):
        act = conv_chunk(slice(j * tf, (j + 1) * tf), wina_ref)
        lin = conv_chunk(slice(d_ff + j * tf, d_ff + (j + 1) * tf), winl_ref)
        gated_ref[:, j * tf:(j + 1) * tf] = (_silu(act) * lin).astype(BF16)

    out = x + mod_ref[5:6, :] * _dot(gated_ref[...], wdown_ref[...])
    if final_norm:
        out = _rms(out, gfin_ref[...])
    o_ref[...] = out


def _conv_ffn(layer, x, mod, g_ffn, w_up, taps, w_down, g_final, final_norm, next_weights):
    b, s, d = x.shape
    tm = TOKEN_TILE
    n_tiles = s // tm
    d_ff = w_down.shape[0]
    tile = pl.BlockSpec((None, tm, d), lambda bi, i: (bi, i, 0))
    cast_in, cast_out, cast_shapes = [], [], []
    for w in next_weights:
        _, rows, cols = w.shape
        least = -(-rows // (b * n_tiles * BF16_ROWS)) * BF16_ROWS
        share = next(r for r in range(least, rows + 1, BF16_ROWS) if rows % r == 0)

        def block(bi, i, last=rows // share - 1):
            return jnp.minimum(bi * n_tiles + i, last)

        cast_in.append(pl.BlockSpec(
            (None, share, cols), lambda bi, i, block=block: (layer + 1, block(bi, i), 0)))
        cast_out.append(pl.BlockSpec(
            (share, cols), lambda bi, i, block=block: (block(bi, i), 0)))
        cast_shapes.append(jax.ShapeDtypeStruct((rows, cols), BF16))
    outs = pl.pallas_call(
        functools.partial(_conv_ffn_kernel, final_norm=final_norm, n_cast=len(next_weights)),
        grid=(b, n_tiles),
        in_specs=[
            tile,
            pl.BlockSpec((None, None, N_MOD, d), lambda bi, i: (layer, bi, 0, 0)),
            _layer_spec(g_ffn.shape, layer),
            _full_spec(w_up.shape),
            _layer_spec(taps.shape, layer),
            _full_spec(w_down.shape),
            _full_spec(g_final.shape),
        ] + cast_in,
        out_specs=[tile] + cast_out,
        out_shape=[jax.ShapeDtypeStruct(x.shape, F32)] + cast_shapes,
        scratch_shapes=[
            pltpu.VMEM((tm, d), BF16),
            pltpu.VMEM((tm, d_ff), BF16),
            pltpu.VMEM((SUBLANES, 2 * d_ff), F32),
            pltpu.VMEM((SUBLANES + tm, FF_CHUNK), F32),
            pltpu.VMEM((SUBLANES + tm, FF_CHUNK), F32),
        ],
        compiler_params=pltpu.CompilerParams(
            dimension_semantics=("arbitrary", "arbitrary"), vmem_limit_bytes=VMEM_LIMIT),
        name="conv_ffn",
    )(x, mod, g_ffn, w_up, taps, w_down, g_final, *next_weights)
    return outs[0], tuple(outs[1:])


def kernel(x, c, w_mod, b_mod, g_mix, w_in, g_sgu, w_sgu_s, b_sgu_s, w_attn_br, w_sgu_br, w_out,
           g_ffn, w_up, w_conv, b_conv, w_down, g_final):
    b, s, d = x.shape
    depth = w_mod.shape[0]
    d_ff = w_down.shape[1]
    assert s % TOKEN_TILE == 0 and TOKEN_TILE % MOBA_BLOCK == 0 and d_ff % FF_CHUNK == 0

    weights_f32 = (w_in, w_attn_br, w_sgu_br, w_out, w_up, w_down)
    weights = tuple(w[0].astype(BF16) for w in weights_f32)
    b_sp = jnp.repeat(jnp.swapaxes(b_sgu_s, 1, 2), SGU_GROUP_DIM, axis=2)
    taps = jnp.concatenate(
        [w_conv, b_conv[:, None, :],
         jnp.zeros((depth, SUBLANES - CONV_WIDTH - 1, 2 * d_ff), F32)], axis=1)
    g_mix3, g_sgu3, g_ffn3 = (g.reshape(depth, 1, -1) for g in (g_mix, g_sgu, g_ffn))
    g_fin = g_final.reshape(1, d)

    mod = _modulation(c, w_mod, b_mod).reshape(depth, b, N_MOD, d)
    for l in range(depth):
        w_in_b, w_attn_b, w_sgu_b, w_out_b, w_up_b, w_down_b = weights
        last = l == depth - 1
        q, k, v, kmean, sgu = _mix_in(l, x, mod, g_mix3, w_in_b, g_sgu3, w_sgu_s, b_sp)
        attn = _moba(q, k, v, kmean.reshape(b, s // MOBA_BLOCK, ATTN_WIDTH))
        x = _mix_out(l, x, mod, g_mix3, attn, sgu, w_in_b, w_attn_b, w_sgu_b, w_out_b)
        x, weights = _conv_ffn(l, x, mod, g_ffn3, w_up_b, taps, w_down_b, g_fin,
                               final_norm=last, next_weights=() if last else weights_f32)
    return x
```

```python
import functools
import math

import jax
import jax.numpy as jnp
from jax import lax
from jax.experimental import pallas as pl
from jax.experimental.pallas import tpu as pltpu

F32 = jnp.float32
BF16 = jnp.bfloat16

N_HEADS = 8
HEAD_DIM = 64
ATTN_WIDTH = N_HEADS * HEAD_DIM
MOBA_BLOCK = 256
MOBA_TOPK = 3
SGU_GROUPS = 8
SGU_GROUP_DIM = 64
SGU_WIDTH = SGU_GROUPS * SGU_GROUP_DIM
SGU_CHUNK = 128
CONV_WIDTH = 3
N_MOD = 6
EPS = 1e-6
QKV_WIDTH = 3 * ATTN_WIDTH
UV_WIDTH = 2 * SGU_WIDTH

LANES = 128
SUBLANES = 8
BF16_ROWS = 16
HEADS_PER_STEP = LANES // HEAD_DIM
VMEM_LIMIT = 56 * 1024 * 1024

TOKEN_TILE = 1024
FF_CHUNK = 256
MOD_COLS = 2048
SCORE_LOOKAHEAD = 3
Q_SCALE = HEAD_DIM ** -0.5 * math.log2(math.e)


def _dot(a, b):
    return jnp.dot(a, b, preferred_element_type=F32)


def _dot_nt(a, b):
    return lax.dot_general(a, b, (((1,), (1,)), ((), ())), preferred_element_type=F32)


def _sigmoid(x):
    return 0.5 + 0.5 * jnp.tanh(0.5 * x)


def _silu(x):
    half = 0.5 * x
    return half + half * jnp.tanh(half)


def _gelu_tanh(x):
    c = 0.7978845608028654
    return 0.5 * x * (1.0 + jnp.tanh(c * (x + 0.044715 * (x * x * x))))


def _rms(x, g):
    return x * lax.rsqrt(jnp.mean(x * x, axis=-1, keepdims=True) + EPS) * g


def _norm_mod(x, g, shift, scale):
    return _rms(x, g * (1.0 + scale)) + shift


def _layer_spec(shape, layer):
    zeros = (0,) * (len(shape) - 1)
    return pl.BlockSpec((None,) + tuple(shape[1:]), lambda *_: (layer,) + zeros)


def _full_spec(shape):
    zeros = (0,) * len(shape)
    return pl.BlockSpec(tuple(shape), lambda *_: zeros)


def _mod_kernel(c_ref, w_ref, b_ref, o_ref):
    c = c_ref[...]
    c_act = _silu(c).astype(BF16)
    o_ref[...] = _dot(c_act, w_ref[...].astype(BF16)) + b_ref[...]


def _modulation(c, w_mod, b_mod):
    depth, d, n = w_mod.shape
    b = c.shape[0]
    return pl.pallas_call(
        _mod_kernel,
        grid=(depth, n // MOD_COLS),
        in_specs=[
            pl.BlockSpec((b, d), lambda l, j: (0, 0)),
            pl.BlockSpec((None, d, MOD_COLS), lambda l, j: (l, 0, j)),
            pl.BlockSpec((None, 1, MOD_COLS), lambda l, j: (l, 0, j)),
        ],
        out_specs=pl.BlockSpec((None, b, MOD_COLS), lambda l, j: (l, 0, j)),
        out_shape=jax.ShapeDtypeStruct((depth, b, n), F32),
        compiler_params=pltpu.CompilerParams(
            dimension_semantics=("arbitrary", "arbitrary"), vmem_limit_bytes=VMEM_LIMIT),
        name="modulation",
    )(c, w_mod, b_mod.reshape(depth, 1, n))


def _mix_in_kernel(x_ref, mod_ref, g_ref, win_ref, gsgu_ref, wsgu_ref, bsp_ref,
                   q_ref, k_ref, v_ref, kmean_ref, sgu_ref):
    tm = x_ref.shape[0]
    h = _norm_mod(x_ref[...], g_ref[...], mod_ref[0:1, :], mod_ref[1:2, :]).astype(BF16)

    uv = _dot(h, win_ref[:, QKV_WIDTH:QKV_WIDTH + UV_WIDTH])
    qk = _dot(h, win_ref[:, :2 * ATTN_WIDTH])
    q_ref[...] = (qk[:, :ATTN_WIDTH] * Q_SCALE).astype(BF16)
    k = qk[:, ATTN_WIDTH:]
    k_ref[...] = k.astype(BF16)
    for c in range(tm // MOBA_BLOCK):
        kmean_ref[c] = jnp.mean(k[c * MOBA_BLOCK:(c + 1) * MOBA_BLOCK], axis=0, keepdims=True)
    v_mid = 2 * ATTN_WIDTH + ATTN_WIDTH // 2
    v_ref[:, :ATTN_WIDTH // 2] = _dot(h, win_ref[:, 2 * ATTN_WIDTH:v_mid]).astype(BF16)

    gu = _gelu_tanh(uv[:, :SGU_WIDTH])
    vn = _rms(_gelu_tanh(uv[:, SGU_WIDTH:]), gsgu_ref[...]).astype(BF16)

    lane = lax.broadcasted_iota(jnp.int32, (SGU_CHUNK, LANES), 1)
    first = lane < SGU_GROUP_DIM
    wrow = lax.broadcasted_iota(jnp.int32, (SGU_CHUNK, 2 * SGU_CHUNK), 0)
    wcol = lax.broadcasted_iota(jnp.int32, (SGU_CHUNK, 2 * SGU_CHUNK), 1)
    causal = jnp.where(wcol >= SGU_CHUNK, wcol - SGU_CHUNK, wcol) <= wrow
    zero = jnp.zeros((), BF16)
    for gp in range(SGU_WIDTH // LANES):
        cols = slice(gp * LANES, (gp + 1) * LANES)
        w = jnp.concatenate([wsgu_ref[2 * gp], wsgu_ref[2 * gp + 1]], axis=1)
        w = jnp.where(causal, w, 0.0).astype(BF16)
        bias = bsp_ref[:, cols]
        for c in range(tm // SGU_CHUNK):
            rows = slice(c * SGU_CHUNK, (c + 1) * SGU_CHUNK)
            vb = vn[rows, cols]
            rhs = jnp.concatenate([jnp.where(first, vb, zero), jnp.where(first, zero, vb)], axis=0)
            mixed = _dot(w, rhs) + bias
            sgu_ref[rows, cols] = (gu[rows, cols] * mixed).astype(BF16)

    v_ref[:, ATTN_WIDTH // 2:] = _dot(h, win_ref[:, v_mid:QKV_WIDTH]).astype(BF16)


def _mix_in(layer, x, mod, g_mix, w_in, g_sgu, w_sgu_s, b_sp):
    b, s, d = x.shape
    tm = TOKEN_TILE
    blocks_per_tile = tm // MOBA_BLOCK
    act = lambda width: jax.ShapeDtypeStruct((b, s, width), BF16)
    act_spec = lambda width: pl.BlockSpec((None, tm, width), lambda bi, i: (bi, i, 0))
    return pl.pallas_call(
        _mix_in_kernel,
        grid=(b, s // tm),
        in_specs=[
            pl.BlockSpec((None, tm, d), lambda bi, i: (bi, i, 0)),
            pl.BlockSpec((None, None, N_MOD, d), lambda bi, i: (layer, bi, 0, 0)),
            _layer_spec(g_mix.shape, layer),
            pl.BlockSpec((d, QKV_WIDTH + UV_WIDTH), lambda bi, i: (0, 0)),
            _layer_spec(g_sgu.shape, layer),
            _layer_spec(w_sgu_s.shape, layer),
            _layer_spec(b_sp.shape, layer),
        ],
        out_specs=[
            act_spec(ATTN_WIDTH), act_spec(ATTN_WIDTH), act_spec(ATTN_WIDTH),
            pl.BlockSpec((None, blocks_per_tile, 1, ATTN_WIDTH), lambda bi, i: (bi, i, 0, 0)),
            act_spec(SGU_WIDTH),
        ],
        out_shape=[
            act(ATTN_WIDTH), act(ATTN_WIDTH), act(ATTN_WIDTH),
            jax.ShapeDtypeStruct((b, s // MOBA_BLOCK, 1, ATTN_WIDTH), F32),
            act(SGU_WIDTH),
        ],
        compiler_params=pltpu.CompilerParams(
            dimension_semantics=("arbitrary", "arbitrary"), vmem_limit_bytes=VMEM_LIMIT),
        name="mix_in",
    )(x, mod, g_mix, w_in, g_sgu, w_sgu_s, b_sp)


def _moba_kernel(q_ref, k_ref, v_ref, km_ref, o_ref, vt_ref, ot_ref):
    nb = km_ref.shape[0]
    s_len = q_ref.shape[0]
    neg_inf = jnp.float32(-jnp.inf)
    blocks = [slice(j * MOBA_BLOCK, (j + 1) * MOBA_BLOCK) for j in range(nb)]

    q = q_ref[...]
    lane = lax.broadcasted_iota(jnp.int32, (s_len, LANES), 1)
    qh = []
    for h in range(HEADS_PER_STEP):
        in_head = (lane >= h * HEAD_DIM) & (lane < (h + 1) * HEAD_DIM)
        qh.append(jnp.where(in_head, q, jnp.zeros((), BF16)))

    def scores(h, i):
        return _dot_nt(k_ref[:(i + 1) * MOBA_BLOCK, :], qh[h][blocks[i]])

    order = [(h, i) for i in range(nb) for h in range(HEADS_PER_STEP)]
    pending = [scores(*order[n]) for n in range(SCORE_LOOKAHEAD)]

    for j in range(nb):
        vt = v_ref[blocks[j], :].astype(F32).T.astype(BF16)
        for h in range(HEADS_PER_STEP):
            vt_ref[h, :HEAD_DIM, blocks[j]] = vt[h * HEAD_DIM:(h + 1) * HEAD_DIM]
    for h in range(HEADS_PER_STEP):
        vt_ref[h, HEAD_DIM:, :] = jnp.ones((BF16_ROWS, s_len), BF16)

    km = km_ref[...]
    km_hi = km.astype(BF16)
    km_lo = (km - km_hi.astype(F32)).astype(BF16)
    blk = lax.broadcasted_iota(jnp.int32, (nb, s_len), 0)
    q_blk = lax.broadcasted_iota(jnp.int32, (nb, s_len), 1) // MOBA_BLOCK
    kpos = lax.broadcasted_iota(jnp.int32, (MOBA_BLOCK, MOBA_BLOCK), 0)
    qpos = lax.broadcasted_iota(jnp.int32, (MOBA_BLOCK, MOBA_BLOCK), 1)
    causal = kpos <= qpos

    bias = []
    for h in range(HEADS_PER_STEP):
        route = _dot_nt(km_hi, qh[h]) + _dot_nt(km_lo, qh[h])
        rank = jnp.zeros((nb, s_len), jnp.int32)
        for jp in range(nb):
            rj = route[jp:jp + 1, :]
            beats = (rj > route) | ((rj == route) & (blk > jp))
            rank = rank + jnp.where(beats & (q_blk > jp), 1, 0)
        bias.append(jnp.where((blk < q_blk) & (rank < MOBA_TOPK), 0.0, neg_inf))

    for n, (h, i) in enumerate(order):
        s = pending.pop(0)
        if n + SCORE_LOOKAHEAD < len(order):
            pending.append(scores(*order[n + SCORE_LOOKAHEAD]))
        parts = [s[blocks[j]] + bias[h][j:j + 1, blocks[i]] for j in range(i)]
        parts.append(jnp.where(causal, s[blocks[i]], neg_inf))
        s = jnp.concatenate(parts, axis=0)
        m = jnp.max(s, axis=0, keepdims=True)
        p = jnp.exp2(s - m).astype(BF16)
        pv = _dot(vt_ref[h, :, :(i + 1) * MOBA_BLOCK], p)
        ot_ref[h * HEAD_DIM:(h + 1) * HEAD_DIM, blocks[i]] = (
            pv[:HEAD_DIM] / pv[HEAD_DIM:HEAD_DIM + 1])
        if h == HEADS_PER_STEP - 1:
            o_ref[blocks[i], :] = ot_ref[:, blocks[i]].T.astype(o_ref.dtype)


def _moba(q, k, v, kmean):
    b, s, width = q.shape
    nb = s // MOBA_BLOCK
    slab = pl.BlockSpec((None, s, LANES), lambda bi, p: (bi, 0, p))
    return pl.pallas_call(
        _moba_kernel,
        grid=(b, width // LANES),
        in_specs=[slab, slab, slab, pl.BlockSpec((None, nb, LANES), lambda bi, p: (bi, 0, p))],
        out_specs=slab,
        out_shape=jax.ShapeDtypeStruct((b, s, width), BF16),
        scratch_shapes=[
            pltpu.VMEM((HEADS_PER_STEP, HEAD_DIM + BF16_ROWS, s), BF16),
            pltpu.VMEM((LANES, s), F32),
        ],
        compiler_params=pltpu.CompilerParams(
            dimension_semantics=("arbitrary", "arbitrary"), vmem_limit_bytes=VMEM_LIMIT),
        name="moba",
    )(q, k, v, kmean)


def _mix_out_kernel(x_ref, mod_ref, g_ref, attn_ref, sgu_ref, win_ref, wa_ref, ws_ref, wo_ref,
                    o_ref):
    d = x_ref.shape[1]
    gate_cols = QKV_WIDTH + UV_WIDTH
    ya = _dot(attn_ref[...], wa_ref[...])
    ys = _dot(sgu_ref[...], ws_ref[...])
    x = x_ref[...]
    h = _norm_mod(x, g_ref[...], mod_ref[0:1, :], mod_ref[1:2, :]).astype(BF16)
    gates = _dot(h, win_ref[:, gate_cols:])
    merged = _sigmoid(gates[:, :d]) * ya + _sigmoid(gates[:, d:]) * ys
    o_ref[...] = x + mod_ref[2:3, :] * _dot(merged.astype(BF16), wo_ref[...])


def _mix_out(layer, x, mod, g_mix, attn, sgu, w_in, w_attn_br, w_sgu_br, w_out):
    b, s, d = x.shape
    tm = TOKEN_TILE
    tile = lambda width: pl.BlockSpec((None, tm, width), lambda bi, i: (bi, i, 0))
    return pl.pallas_call(
        _mix_out_kernel,
        grid=(b, s // tm),
        in_specs=[
            tile(d),
            pl.BlockSpec((None, None, N_MOD, d), lambda bi, i: (layer, bi, 0, 0)),
            _layer_spec(g_mix.shape, layer),
            tile(attn.shape[2]),
            tile(sgu.shape[2]),
            _full_spec(w_in.shape),
            _full_spec(w_attn_br.shape),
            _full_spec(w_sgu_br.shape),
            _full_spec(w_out.shape),
        ],
        out_specs=tile(d),
        out_shape=jax.ShapeDtypeStruct(x.shape, F32),
        compiler_params=pltpu.CompilerParams(
            dimension_semantics=("arbitrary", "arbitrary"), vmem_limit_bytes=VMEM_LIMIT),
        name="mix_out",
    )(x, mod, g_mix, attn, sgu, w_in, w_attn_br, w_sgu_br, w_out)


def _conv_ffn_kernel(*refs, final_norm, n_cast):
    (x_ref, mod_ref, g_ref, wup_ref, taps_ref, wdown_ref, gfin_ref), refs = refs[:7], refs[7:]
    cast_src, refs = refs[:n_cast], refs[n_cast:]
    o_ref, refs = refs[0], refs[1:]
    cast_dst, refs = refs[:n_cast], refs[n_cast:]
    h_ref, gated_ref, tail_ref, wina_ref, winl_ref = refs

    i = pl.program_id(1)
    tm = x_ref.shape[0]
    d_ff = wdown_ref.shape[0]
    tf = FF_CHUNK
    x = x_ref[...]
    h_ref[...] = _norm_mod(x, g_ref[...], mod_ref[3:4, :], mod_ref[4:5, :]).astype(BF16)

    for src, dst in zip(cast_src, cast_dst):
        dst[...] = src[...].astype(BF16)

    def conv_chunk(cols, win_ref):
        up = _dot(h_ref[...], wup_ref[:, cols])
        win_ref[:SUBLANES, :] = jnp.where(i > 0, tail_ref[:, cols], 0.0)
        win_ref[SUBLANES:, :] = up
        tail_ref[:, cols] = up[tm - SUBLANES:, :]
        taps = taps_ref[:, cols]
        conv = taps[3:4, :] + taps[2:3, :] * up
        for back in (1, 2):
            conv = conv + taps[2 - back:3 - back, :] * win_ref[pl.ds(SUBLANES - back, tm), :]
        return conv

    for j in range(d_ff // tf):
        act = conv_chunk(slice(j * tf, (j + 1) * tf), wina_ref)
        lin = conv_chunk(slice(d_ff + j * tf, d_ff + (j + 1) * tf), winl_ref)
        gated_ref[:, j * tf:(j + 1) * tf] = (_silu(act) * lin).astype(BF16)

    out = x + mod_ref[5:6, :] * _dot(gated_ref[...], wdown_ref[...])
    if final_norm:
        out = _rms(out, gfin_ref[...])
    o_ref[...] = out


def _conv_ffn(layer, x, mod, g_ffn, w_up, taps, w_down, g_final, final_norm, next_weights):
    b, s, d = x.shape
    tm = TOKEN_TILE
    n_tiles = s // tm
    d_ff = w_down.shape[0]
    tile = pl.BlockSpec((None, tm, d), lambda bi, i: (bi, i, 0))
    cast_in, cast_out, cast_shapes = [], [], []
    for w in next_weights:
        _, rows, cols = w.shape
        least = -(-rows // (b * n_tiles * BF16_ROWS)) * BF16_ROWS
        share = next(r for r in range(least, rows + 1, BF16_ROWS) if rows % r == 0)

        def block(bi, i, last=rows // share - 1):
            return jnp.minimum(bi * n_tiles + i, last)

        cast_in.append(pl.BlockSpec(
            (None, share, cols), lambda bi, i, block=block: (layer + 1, block(bi, i), 0)))
        cast_out.append(pl.BlockSpec(
            (share, cols), lambda bi, i, block=block: (block(bi, i), 0)))
        cast_shapes.append(jax.ShapeDtypeStruct((rows, cols), BF16))
    outs = pl.pallas_call(
        functools.partial(_conv_ffn_kernel, final_norm=final_norm, n_cast=len(next_weights)),
        grid=(b, n_tiles),
        in_specs=[
            tile,
            pl.BlockSpec((None, None, N_MOD, d), lambda bi, i: (layer, bi, 0, 0)),
            _layer_spec(g_ffn.shape, layer),
            _full_spec(w_up.shape),
            _layer_spec(taps.shape, layer),
            _full_spec(w_down.shape),
            _full_spec(g_final.shape),
        ] + cast_in,
        out_specs=[tile] + cast_out,
        out_shape=[jax.ShapeDtypeStruct(x.shape, F32)] + cast_shapes,
        scratch_shapes=[
            pltpu.VMEM((tm, d), BF16),
            pltpu.VMEM((tm, d_ff), BF16),
            pltpu.VMEM((SUBLANES, 2 * d_ff), F32),
            pltpu.VMEM((SUBLANES + tm, FF_CHUNK), F32),
            pltpu.VMEM((SUBLANES + tm, FF_CHUNK), F32),
        ],
        compiler_params=pltpu.CompilerParams(
            dimension_semantics=("arbitrary", "arbitrary"), vmem_limit_bytes=VMEM_LIMIT),
        name="conv_ffn",
    )(x, mod, g_ffn, w_up, taps, w_down, g_final, *next_weights)
    return outs[0], tuple(outs[1:])


def kernel(x, c, w_mod, b_mod, g_mix, w_in, g_sgu, w_sgu_s, b_sgu_s, w_attn_br, w_sgu_br, w_out,
           g_ffn, w_up, w_conv, b_conv, w_down, g_final):
    b, s, d = x.shape
    depth = w_mod.shape[0]
    d_ff = w_down.shape[1]
    assert s % TOKEN_TILE == 0 and TOKEN_TILE % MOBA_BLOCK == 0 and d_ff % FF_CHUNK == 0

    weights_f32 = (w_in, w_attn_br, w_sgu_br, w_out, w_up, w_down)
    weights = tuple(w[0].astype(BF16) for w in weights_f32)
    b_sp = jnp.repeat(jnp.swapaxes(b_sgu_s, 1, 2), SGU_GROUP_DIM, axis=2)
    taps = jnp.concatenate(
        [w_conv, b_conv[:, None, :],
         jnp.zeros((depth, SUBLANES - CONV_WIDTH - 1, 2 * d_ff), F32)], axis=1)
    g_mix3, g_sgu3, g_ffn3 = (g.reshape(depth, 1, -1) for g in (g_mix, g_sgu, g_ffn))
    g_fin = g_final.reshape(1, d)

    mod = _modulation(c, w_mod, b_mod).reshape(depth, b, N_MOD, d)
    for l in range(depth):
        w_in_b, w_attn_b, w_sgu_b, w_out_b, w_up_b, w_down_b = weights
        last = l == depth - 1
        q, k, v, kmean, sgu = _mix_in(l, x, mod, g_mix3, w_in_b, g_sgu3, w_sgu_s, b_sp)
        attn = _moba(q, k, v, kmean.reshape(b, s // MOBA_BLOCK, ATTN_WIDTH))
        x = _mix_out(l, x, mod, g_mix3, attn, sgu, w_in_b, w_attn_b, w_sgu_b, w_out_b)
        x, weights = _conv_ffn(l, x, mod, g_ffn3, w_up_b, taps, w_down_b, g_fin,
                               final_norm=last, next_weights=() if last else weights_f32)
    return x
```

```python
import functools
import math

import jax
import jax.numpy as jnp
from jax import lax
from jax.experimental import pallas as pl
from jax.experimental.pallas import tpu as pltpu

F32 = jnp.float32
BF16 = jnp.bfloat16

N_HEADS = 8
HEAD_DIM = 64
ATTN_WIDTH = N_HEADS * HEAD_DIM
MOBA_BLOCK = 256
MOBA_TOPK = 3
SGU_GROUPS = 8
SGU_GROUP_DIM = 64
SGU_WIDTH = SGU_GROUPS * SGU_GROUP_DIM
SGU_CHUNK = 128
CONV_WIDTH = 3
N_MOD = 6
EPS = 1e-6
QKV_WIDTH = 3 * ATTN_WIDTH
UV_WIDTH = 2 * SGU_WIDTH

LANES = 128
SUBLANES = 8
BF16_ROWS = 16
HEADS_PER_STEP = LANES // HEAD_DIM
VMEM_LIMIT = 56 * 1024 * 1024

TOKEN_TILE = 1024
FF_CHUNK = 256
MOD_COLS = 2048
SCORE_LOOKAHEAD = 3
Q_SCALE = HEAD_DIM ** -0.5 * math.log2(math.e)


def _dot(a, b):
    return jnp.dot(a, b, preferred_element_type=F32)


def _dot_nt(a, b):
    return lax.dot_general(a, b, (((1,), (1,)), ((), ())), preferred_element_type=F32)


def _sigmoid(x):
    return 0.5 + 0.5 * jnp.tanh(0.5 * x)


def _silu(x):
    half = 0.5 * x
    return half + half * jnp.tanh(half)


def _gelu_tanh(x):
    c = 0.7978845608028654
    return 0.5 * x * (1.0 + jnp.tanh(c * (x + 0.044715 * (x * x * x))))


def _rms(x, g):
    return x * lax.rsqrt(jnp.mean(x * x, axis=-1, keepdims=True) + EPS) * g


def _norm_mod(x, g, shift, scale):
    return _rms(x, g * (1.0 + scale)) + shift


def _layer_spec(shape, layer):
    zeros = (0,) * (len(shape) - 1)
    return pl.BlockSpec((None,) + tuple(shape[1:]), lambda *_: (layer,) + zeros)


def _full_spec(shape):
    zeros = (0,) * len(shape)
    return pl.BlockSpec(tuple(shape), lambda *_: zeros)


def _mod_kernel(c_ref, w_ref, b_ref, o_ref):
    c = c_ref[...]
    c_act = _silu(c).astype(BF16)
    o_ref[...] = _dot(c_act, w_ref[...].astype(BF16)) + b_ref[...]


def _modulation(c, w_mod, b_mod):
    depth, d, n = w_mod.shape
    b = c.shape[0]
    return pl.pallas_call(
        _mod_kernel,
        grid=(depth, n // MOD_COLS),
        in_specs=[
            pl.BlockSpec((b, d), lambda l, j: (0, 0)),
            pl.BlockSpec((None, d, MOD_COLS), lambda l, j: (l, 0, j)),
            pl.BlockSpec((None, 1, MOD_COLS), lambda l, j: (l, 0, j)),
        ],
        out_specs=pl.BlockSpec((None, b, MOD_COLS), lambda l, j: (l, 0, j)),
        out_shape=jax.ShapeDtypeStruct((depth, b, n), F32),
        compiler_params=pltpu.CompilerParams(
            dimension_semantics=("arbitrary", "arbitrary"), vmem_limit_bytes=VMEM_LIMIT),
        name="modulation",
    )(c, w_mod, b_mod.reshape(depth, 1, n))


def _mix_in_kernel(x_ref, mod_ref, g_ref, win_ref, gsgu_ref, wsgu_ref, bsp_ref,
                   q_ref, k_ref, v_ref, kmean_ref, sgu_ref):
    tm = x_ref.shape[0]
    h = _norm_mod(x_ref[...], g_ref[...], mod_ref[0:1, :], mod_ref[1:2, :]).astype(BF16)

    uv = _dot(h, win_ref[:, QKV_WIDTH:QKV_WIDTH + UV_WIDTH])
    qk = _dot(h, win_ref[:, :2 * ATTN_WIDTH])
    q_ref[...] = (qk[:, :ATTN_WIDTH] * Q_SCALE).astype(BF16)
    k = qk[:, ATTN_WIDTH:]
    k_ref[...] = k.astype(BF16)
    for c in range(tm // MOBA_BLOCK):
        kmean_ref[c] = jnp.mean(k[c * MOBA_BLOCK:(c + 1) * MOBA_BLOCK], axis=0, keepdims=True)
    v_mid = 2 * ATTN_WIDTH + ATTN_WIDTH // 2
    v_ref[:, :ATTN_WIDTH // 2] = _dot(h, win_ref[:, 2 * ATTN_WIDTH:v_mid]).astype(BF16)

    gu = _gelu_tanh(uv[:, :SGU_WIDTH])
    vn = _rms(_gelu_tanh(uv[:, SGU_WIDTH:]), gsgu_ref[...]).astype(BF16)

    lane = lax.broadcasted_iota(jnp.int32, (SGU_CHUNK, LANES), 1)
    first = lane < SGU_GROUP_DIM
    wrow = lax.broadcasted_iota(jnp.int32, (SGU_CHUNK, 2 * SGU_CHUNK), 0)
    wcol = lax.broadcasted_iota(jnp.int32, (SGU_CHUNK, 2 * SGU_CHUNK), 1)
    causal = jnp.where(wcol >= SGU_CHUNK, wcol - SGU_CHUNK, wcol) <= wrow
    zero = jnp.zeros((), BF16)
    chunks = [slice(c * SGU_CHUNK, (c + 1) * SGU_CHUNK) for c in range(tm // SGU_CHUNK)]
    for gp in range(SGU_WIDTH // LANES):
        cols = slice(gp * LANES, (gp + 1) * LANES)
        w = jnp.concatenate([wsgu_ref[2 * gp], wsgu_ref[2 * gp + 1]], axis=1)
        w = jnp.where(causal, w, 0.0).astype(BF16)
        bias = bsp_ref[:, cols]
        rhs = jnp.concatenate(
            [jnp.concatenate([jnp.where(first, vn[rows, cols], zero),
                              jnp.where(first, zero, vn[rows, cols])], axis=0)
             for rows in chunks], axis=1)
        mixed = _dot(w, rhs)
        for c, rows in enumerate(chunks):
            sgu_ref[rows, cols] = (
                gu[rows, cols] * (mixed[:, c * LANES:(c + 1) * LANES] + bias)).astype(BF16)

    v_ref[:, ATTN_WIDTH // 2:] = _dot(h, win_ref[:, v_mid:QKV_WIDTH]).astype(BF16)


def _mix_in(layer, x, mod, g_mix, w_in, g_sgu, w_sgu_s, b_sp):
    b, s, d = x.shape
    tm = TOKEN_TILE
    blocks_per_tile = tm // MOBA_BLOCK
    act = lambda width: jax.ShapeDtypeStruct((b, s, width), BF16)
    act_spec = lambda width: pl.BlockSpec((None, tm, width), lambda bi, i: (bi, i, 0))
    return pl.pallas_call(
        _mix_in_kernel,
        grid=(b, s // tm),
        in_specs=[
            pl.BlockSpec((None, tm, d), lambda bi, i: (bi, i, 0)),
            pl.BlockSpec((None, None, N_MOD, d), lambda bi, i: (layer, bi, 0, 0)),
            _layer_spec(g_mix.shape, layer),
            pl.BlockSpec((d, QKV_WIDTH + UV_WIDTH), lambda bi, i: (0, 0)),
            _layer_spec(g_sgu.shape, layer),
            _layer_spec(w_sgu_s.shape, layer),
            _layer_spec(b_sp.shape, layer),
        ],
        out_specs=[
            act_spec(ATTN_WIDTH), act_spec(ATTN_WIDTH), act_spec(ATTN_WIDTH),
            pl.BlockSpec((None, blocks_per_tile, 1, ATTN_WIDTH), lambda bi, i: (bi, i, 0, 0)),
            act_spec(SGU_WIDTH),
        ],
        out_shape=[
            act(ATTN_WIDTH), act(ATTN_WIDTH), act(ATTN_WIDTH),
            jax.ShapeDtypeStruct((b, s // MOBA_BLOCK, 1, ATTN_WIDTH), F32),
            act(SGU_WIDTH),
        ],
        compiler_params=pltpu.CompilerParams(
            dimension_semantics=("arbitrary", "arbitrary"), vmem_limit_bytes=VMEM_LIMIT),
        name="mix_in",
    )(x, mod, g_mix, w_in, g_sgu, w_sgu_s, b_sp)


def _moba_kernel(q_ref, k_ref, v_ref, km_ref, o_ref, vt_ref, ot_ref):
    nb = km_ref.shape[0]
    s_len = q_ref.shape[0]
    neg_inf = jnp.float32(-jnp.inf)
    blocks = [slice(j * MOBA_BLOCK, (j + 1) * MOBA_BLOCK) for j in range(nb)]

    q = q_ref[...]
    lane = lax.broadcasted_iota(jnp.int32, (s_len, LANES), 1)
    qh = []
    for h in range(HEADS_PER_STEP):
        in_head = (lane >= h * HEAD_DIM) & (lane < (h + 1) * HEAD_DIM)
        qh.append(jnp.where(in_head, q, jnp.zeros((), BF16)))

    def scores(h, i):
        return _dot_nt(k_ref[:(i + 1) * MOBA_BLOCK, :], qh[h][blocks[i]])

    order = [(h, i) for i in range(nb) for h in range(HEADS_PER_STEP)]
    pending = [scores(*order[n]) for n in range(SCORE_LOOKAHEAD)]

    for j in range(nb):
        vt = v_ref[blocks[j], :].astype(F32).T.astype(BF16)
        for h in range(HEADS_PER_STEP):
            vt_ref[h, :HEAD_DIM, blocks[j]] = vt[h * HEAD_DIM:(h + 1) * HEAD_DIM]
    for h in range(HEADS_PER_STEP):
        vt_ref[h, HEAD_DIM:, :] = jnp.ones((BF16_ROWS, s_len), BF16)

    km = km_ref[...]
    km_hi = km.astype(BF16).astype(F32)
    km_lane = lax.broadcasted_iota(jnp.int32, (nb, LANES), 1)
    km_parts = []
    for h in range(HEADS_PER_STEP):
        in_head = (km_lane >= h * HEAD_DIM) & (km_lane < (h + 1) * HEAD_DIM)
        km_parts += [jnp.where(in_head, km_hi, 0.0), jnp.where(in_head, km - km_hi, 0.0)]
    route_parts = _dot_nt(jnp.concatenate(km_parts, axis=0).astype(BF16), q)
    blk = lax.broadcasted_iota(jnp.int32, (nb, s_len), 0)
    q_blk = lax.broadcasted_iota(jnp.int32, (nb, s_len), 1) // MOBA_BLOCK
    kpos = lax.broadcasted_iota(jnp.int32, (MOBA_BLOCK, MOBA_BLOCK), 0)
    qpos = lax.broadcasted_iota(jnp.int32, (MOBA_BLOCK, MOBA_BLOCK), 1)
    causal = kpos <= qpos

    bias = []
    for h in range(HEADS_PER_STEP):
        route = (route_parts[2 * h * nb:(2 * h + 1) * nb]
                 + route_parts[(2 * h + 1) * nb:(2 * h + 2) * nb])
        rank = jnp.zeros((nb, s_len), jnp.int32)
        for jp in range(nb):
            rj = route[jp:jp + 1, :]
            beats = (rj > route) | ((rj == route) & (blk > jp))
            rank = rank + jnp.where(beats & (q_blk > jp), 1, 0)
        bias.append(jnp.where((blk < q_blk) & (rank < MOBA_TOPK), 0.0, neg_inf))

    for n, (h, i) in enumerate(order):
        s = pending.pop(0)
        if n + SCORE_LOOKAHEAD < len(order):
            pending.append(scores(*order[n + SCORE_LOOKAHEAD]))
        parts = [s[blocks[j]] + bias[h][j:j + 1, blocks[i]] for j in range(i)]
        parts.append(jnp.where(causal, s[blocks[i]], neg_inf))
        s = jnp.concatenate(parts, axis=0)
        m = jnp.max(s, axis=0, keepdims=True)
        p = jnp.exp2(s - m).astype(BF16)
        pv = _dot(vt_ref[h, :, :(i + 1) * MOBA_BLOCK], p)
        ot_ref[h * HEAD_DIM:(h + 1) * HEAD_DIM, blocks[i]] = (
            pv[:HEAD_DIM] / pv[HEAD_DIM:HEAD_DIM + 1])
        if h == HEADS_PER_STEP - 1:
            o_ref[blocks[i], :] = ot_ref[:, blocks[i]].T.astype(o_ref.dtype)


def _moba(q, k, v, kmean):
    b, s, width = q.shape
    nb = s // MOBA_BLOCK
    slab = pl.BlockSpec((None, s, LANES), lambda bi, p: (bi, 0, p))
    return pl.pallas_call(
        _moba_kernel,
        grid=(b, width // LANES),
        in_specs=[slab, slab, slab, pl.BlockSpec((None, nb, LANES), lambda bi, p: (bi, 0, p))],
        out_specs=slab,
        out_shape=jax.ShapeDtypeStruct((b, s, width), BF16),
        scratch_shapes=[
            pltpu.VMEM((HEADS_PER_STEP, HEAD_DIM + BF16_ROWS, s), BF16),
            pltpu.VMEM((LANES, s), F32),
        ],
        compiler_params=pltpu.CompilerParams(
            dimension_semantics=("arbitrary", "arbitrary"), vmem_limit_bytes=VMEM_LIMIT),
        name="moba",
    )(q, k, v, kmean)


def _mix_out_kernel(x_ref, mod_ref, g_ref, attn_ref, sgu_ref, win_ref, wa_ref, ws_ref, wo_ref,
                    o_ref):
    d = x_ref.shape[1]
    gate_cols = QKV_WIDTH + UV_WIDTH
    ya = _dot(attn_ref[...], wa_ref[...])
    ys = _dot(sgu_ref[...], ws_ref[...])
    x = x_ref[...]
    h = _norm_mod(x, g_ref[...], mod_ref[0:1, :], mod_ref[1:2, :]).astype(BF16)
    gates = _dot(h, win_ref[:, gate_cols:])
    merged = _sigmoid(gates[:, :d]) * ya + _sigmoid(gates[:, d:]) * ys
    o_ref[...] = x + mod_ref[2:3, :] * _dot(merged.astype(BF16), wo_ref[...])


def _mix_out(layer, x, mod, g_mix, attn, sgu, w_in, w_attn_br, w_sgu_br, w_out):
    b, s, d = x.shape
    tm = TOKEN_TILE
    tile = lambda width: pl.BlockSpec((None, tm, width), lambda bi, i: (bi, i, 0))
    return pl.pallas_call(
        _mix_out_kernel,
        grid=(b, s // tm),
        in_specs=[
            tile(d),
            pl.BlockSpec((None, None, N_MOD, d), lambda bi, i: (layer, bi, 0, 0)),
            _layer_spec(g_mix.shape, layer),
            tile(attn.shape[2]),
            tile(sgu.shape[2]),
            _full_spec(w_in.shape),
            _full_spec(w_attn_br.shape),
            _full_spec(w_sgu_br.shape),
            _full_spec(w_out.shape),
        ],
        out_specs=tile(d),
        out_shape=jax.ShapeDtypeStruct(x.shape, F32),
        compiler_params=pltpu.CompilerParams(
            dimension_semantics=("arbitrary", "arbitrary"), vmem_limit_bytes=VMEM_LIMIT),
        name="mix_out",
    )(x, mod, g_mix, attn, sgu, w_in, w_attn_br, w_sgu_br, w_out)


def _conv_ffn_kernel(*refs, final_norm, n_cast):
    (x_ref, mod_ref, g_ref, wup_ref, taps_ref, wdown_ref, gfin_ref), refs = refs[:7], refs[7:]
    cast_src, refs = refs[:n_cast], refs[n_cast:]
    o_ref, refs = refs[0], refs[1:]
    cast_dst, refs = refs[:n_cast], refs[n_cast:]
    h_ref, gated_ref, tail_ref, wina_ref, winl_ref = refs

    i = pl.program_id(1)
    tm = x_ref.shape[0]
    d_ff = wdown_ref.shape[0]
    tf = FF_CHUNK
    x = x_ref[...]
    h_ref[...] = _norm_mod(x, g_ref[...], mod_ref[3:4, :], mod_ref[4:5, :]).astype(BF16)

    for src, dst in zip(cast_src, cast_dst):
        dst[...] = src[...].astype(BF16)

    def conv_chunk(cols, win_ref):
        up = _dot(h_ref[...], wup_ref[:, cols])
        win_ref[:SUBLANES, :] = jnp.where(i > 0, tail_ref[:, cols], 0.0)
        win_ref[SUBLANES:, :] = up
        tail_ref[:, cols] = up[tm - SUBLANES:, :]
        taps = taps_ref[:, cols]
        conv = taps[3:4, :] + taps[2:3, :] * up
        for back in (1, 2):
            conv = conv + taps[2 - back:3 - back, :] * win_ref[pl.ds(SUBLANES - back, tm), :]
        return conv

    for j in range(d_ff // tf):
        act = conv_chunk(slice(j * tf, (j + 1) * tf), wina_ref)
        lin = conv_chunk(slice(d_ff + j * tf, d_ff + (j + 1) * tf), winl_ref)
        gated_ref[:, j * tf:(j + 1) * tf] = (_silu(act) * lin).astype(BF16)

    out = x + mod_ref[5:6, :] * _dot(gated_ref[...], wdown_ref[...])
    if final_norm:
        out = _rms(out, gfin_ref[...])
    o_ref[...] = out


def _conv_ffn(layer, x, mod, g_ffn, w_up, taps, w_down, g_final, final_norm, next_weights):
    b, s, d = x.shape
    tm = TOKEN_TILE
    n_tiles = s // tm
    d_ff = w_down.shape[0]
    tile = pl.BlockSpec((None, tm, d), lambda bi, i: (bi, i, 0))
    cast_in, cast_out, cast_shapes = [], [], []
    for w in next_weights:
        _, rows, cols = w.shape
        least = -(-rows // (b * n_tiles * BF16_ROWS)) * BF16_ROWS
        share = next(r for r in range(least, rows + 1, BF16_ROWS) if rows % r == 0)

        def block(bi, i, last=rows // share - 1):
            return jnp.minimum(bi * n_tiles + i, last)

        cast_in.append(pl.BlockSpec(
            (None, share, cols), lambda bi, i, block=block: (layer + 1, block(bi, i), 0)))
        cast_out.append(pl.BlockSpec(
            (share, cols), lambda bi, i, block=block: (block(bi, i), 0)))
        cast_shapes.append(jax.ShapeDtypeStruct((rows, cols), BF16))
    outs = pl.pallas_call(
        functools.partial(_conv_ffn_kernel, final_norm=final_norm, n_cast=len(next_weights)),
        grid=(b, n_tiles),
        in_specs=[
            tile,
            pl.BlockSpec((None, None, N_MOD, d), lambda bi, i: (layer, bi, 0, 0)),
            _layer_spec(g_ffn.shape, layer),
            _full_spec(w_up.shape),
            _layer_spec(taps.shape, layer),
            _full_spec(w_down.shape),
            _full_spec(g_final.shape),
        ] + cast_in,
        out_specs=[tile] + cast_out,
        out_shape=[jax.ShapeDtypeStruct(x.shape, F32)] + cast_shapes,
        scratch_shapes=[
            pltpu.VMEM((tm, d), BF16),
            pltpu.VMEM((tm, d_ff), BF16),
            pltpu.VMEM((SUBLANES, 2 * d_ff), F32),
            pltpu.VMEM((SUBLANES + tm, FF_CHUNK), F32),
            pltpu.VMEM((SUBLANES + tm, FF_CHUNK), F32),
        ],
        compiler_params=pltpu.CompilerParams(
            dimension_semantics=("arbitrary", "arbitrary"), vmem_limit_bytes=VMEM_LIMIT),
        name="conv_ffn",
    )(x, mod, g_ffn, w_up, taps, w_down, g_final, *next_weights)
    return outs[0], tuple(outs[1:])


def kernel(x, c, w_mod, b_mod, g_mix, w_in, g_sgu, w_sgu_s, b_sgu_s, w_attn_br, w_sgu_br, w_out,
           g_ffn, w_up, w_conv, b_conv, w_down, g_final):
    b, s, d = x.shape
    depth = w_mod.shape[0]
    d_ff = w_down.shape[1]
    assert s % TOKEN_TILE == 0 and TOKEN_TILE % MOBA_BLOCK == 0 and d_ff % FF_CHUNK == 0

    weights_f32 = (w_in, w_attn_br, w_sgu_br, w_out, w_up, w_down)
    weights = tuple(w[0].astype(BF16) for w in weights_f32)
    b_sp = jnp.repeat(jnp.swapaxes(b_sgu_s, 1, 2), SGU_GROUP_DIM, axis=2)
    taps = jnp.concatenate(
        [w_conv, b_conv[:, None, :],
         jnp.zeros((depth, SUBLANES - CONV_WIDTH - 1, 2 * d_ff), F32)], axis=1)
    g_mix3, g_sgu3, g_ffn3 = (g.reshape(depth, 1, -1) for g in (g_mix, g_sgu, g_ffn))
    g_fin = g_final.reshape(1, d)

    mod = _modulation(c, w_mod, b_mod).reshape(depth, b, N_MOD, d)
    for l in range(depth):
        w_in_b, w_attn_b, w_sgu_b, w_out_b, w_up_b, w_down_b = weights
        last = l == depth - 1
        q, k, v, kmean, sgu = _mix_in(l, x, mod, g_mix3, w_in_b, g_sgu3, w_sgu_s, b_sp)
        attn = _moba(q, k, v, kmean.reshape(b, s // MOBA_BLOCK, ATTN_WIDTH))
        x = _mix_out(l, x, mod, g_mix3, attn, sgu, w_in_b, w_attn_b, w_sgu_b, w_out_b)
        x, weights = _conv_ffn(l, x, mod, g_ffn3, w_up_b, taps, w_down_b, g_fin,
                               final_norm=last, next_weights=() if last else weights_f32)
    return x
```

```python
import functools
import math

import jax
import jax.numpy as jnp
from jax import lax
from jax.experimental import pallas as pl
from jax.experimental.pallas import tpu as pltpu

F32 = jnp.float32
BF16 = jnp.bfloat16

N_HEADS = 8
HEAD_DIM = 64
ATTN_WIDTH = N_HEADS * HEAD_DIM
MOBA_BLOCK = 256
MOBA_TOPK = 3
SGU_GROUPS = 8
SGU_GROUP_DIM = 64
SGU_WIDTH = SGU_GROUPS * SGU_GROUP_DIM
SGU_CHUNK = 128
CONV_WIDTH = 3
N_MOD = 6
EPS = 1e-6
QKV_WIDTH = 3 * ATTN_WIDTH
UV_WIDTH = 2 * SGU_WIDTH

LANES = 128
SUBLANES = 8
BF16_ROWS = 16
HEADS_PER_STEP = LANES // HEAD_DIM
VMEM_LIMIT = 56 * 1024 * 1024

TOKEN_TILE = 1024
FF_CHUNK = 256
MOD_COLS = 2048
SCORE_LOOKAHEAD = 3
Q_SCALE = HEAD_DIM ** -0.5 * math.log2(math.e)


def _dot(a, b):
    return jnp.dot(a, b, preferred_element_type=F32)


def _dot_nt(a, b):
    return lax.dot_general(a, b, (((1,), (1,)), ((), ())), preferred_element_type=F32)


def _sigmoid(x):
    return 0.5 + 0.5 * jnp.tanh(0.5 * x)


def _silu(x):
    half = 0.5 * x
    return half + half * jnp.tanh(half)


def _gelu_tanh(x):
    c = 0.7978845608028654
    return 0.5 * x * (1.0 + jnp.tanh(c * (x + 0.044715 * (x * x * x))))


def _rms(x, g):
    return x * lax.rsqrt(jnp.mean(x * x, axis=-1, keepdims=True) + EPS) * g


def _norm_mod(x, g, shift, scale):
    return _rms(x, g * (1.0 + scale)) + shift


def _layer_spec(shape, layer):
    zeros = (0,) * (len(shape) - 1)
    return pl.BlockSpec((None,) + tuple(shape[1:]), lambda *_: (layer,) + zeros)


def _full_spec(shape):
    zeros = (0,) * len(shape)
    return pl.BlockSpec(tuple(shape), lambda *_: zeros)


def _cast_job(weights, layer, grid):
    n_steps = grid[0] * grid[1]
    in_specs, out_specs, out_shapes = [], [], []
    for w in weights:
        _, rows, cols = w.shape
        least = -(-rows // (n_steps * BF16_ROWS)) * BF16_ROWS
        share = next(r for r in range(least, rows + 1, BF16_ROWS) if rows % r == 0)

        def block(g0, g1, last=rows // share - 1):
            return jnp.minimum(g0 * grid[1] + g1, last)

        in_specs.append(pl.BlockSpec(
            (None, share, cols), lambda g0, g1, block=block: (layer, block(g0, g1), 0)))
        out_specs.append(pl.BlockSpec(
            (share, cols), lambda g0, g1, block=block: (block(g0, g1), 0)))
        out_shapes.append(jax.ShapeDtypeStruct((rows, cols), BF16))
    return in_specs, out_specs, out_shapes


def _run_cast_job(src_refs, dst_refs):
    for src, dst in zip(src_refs, dst_refs):
        dst[...] = src[...].astype(BF16)


def _mod_kernel(c_ref, w_ref, b_ref, o_ref):
    c = c_ref[...]
    c_act = _silu(c).astype(BF16)
    o_ref[...] = _dot(c_act, w_ref[...].astype(BF16)) + b_ref[...]


def _modulation(c, w_mod, b_mod):
    depth, d, n = w_mod.shape
    b = c.shape[0]
    return pl.pallas_call(
        _mod_kernel,
        grid=(depth, n // MOD_COLS),
        in_specs=[
            pl.BlockSpec((b, d), lambda l, j: (0, 0)),
            pl.BlockSpec((None, d, MOD_COLS), lambda l, j: (l, 0, j)),
            pl.BlockSpec((None, 1, MOD_COLS), lambda l, j: (l, 0, j)),
        ],
        out_specs=pl.BlockSpec((None, b, MOD_COLS), lambda l, j: (l, 0, j)),
        out_shape=jax.ShapeDtypeStruct((depth, b, n), F32),
        compiler_params=pltpu.CompilerParams(
            dimension_semantics=("arbitrary", "arbitrary"), vmem_limit_bytes=VMEM_LIMIT),
        name="modulation",
    )(c, w_mod, b_mod.reshape(depth, 1, n))


def _mix_in_kernel(x_ref, mod_ref, g_ref, win_ref, gsgu_ref, wsgu_ref, bsp_ref,
                   q_ref, k_ref, v_ref, kmean_ref, sgu_ref):
    tm = x_ref.shape[0]
    h = _norm_mod(x_ref[...], g_ref[...], mod_ref[0:1, :], mod_ref[1:2, :]).astype(BF16)

    uv = _dot(h, win_ref[:, QKV_WIDTH:QKV_WIDTH + UV_WIDTH])
    qk = _dot(h, win_ref[:, :2 * ATTN_WIDTH])
    q_ref[...] = (qk[:, :ATTN_WIDTH] * Q_SCALE).astype(BF16)
    k = qk[:, ATTN_WIDTH:]
    k_ref[...] = k.astype(BF16)
    for c in range(tm // MOBA_BLOCK):
        kmean_ref[c] = jnp.mean(k[c * MOBA_BLOCK:(c + 1) * MOBA_BLOCK], axis=0, keepdims=True)
    v_mid = 2 * ATTN_WIDTH + ATTN_WIDTH // 2
    v_ref[:, :ATTN_WIDTH // 2] = _dot(h, win_ref[:, 2 * ATTN_WIDTH:v_mid]).astype(BF16)

    gu = _gelu_tanh(uv[:, :SGU_WIDTH])
    vn = _rms(_gelu_tanh(uv[:, SGU_WIDTH:]), gsgu_ref[...]).astype(BF16)

    lane = lax.broadcasted_iota(jnp.int32, (SGU_CHUNK, LANES), 1)
    first = lane < SGU_GROUP_DIM
    wrow = lax.broadcasted_iota(jnp.int32, (SGU_CHUNK, 2 * SGU_CHUNK), 0)
    wcol = lax.broadcasted_iota(jnp.int32, (SGU_CHUNK, 2 * SGU_CHUNK), 1)
    causal = jnp.where(wcol >= SGU_CHUNK, wcol - SGU_CHUNK, wcol) <= wrow
    zero = jnp.zeros((), BF16)
    chunks = [slice(c * SGU_CHUNK, (c + 1) * SGU_CHUNK) for c in range(tm // SGU_CHUNK)]
    for gp in range(SGU_WIDTH // LANES):
        cols = slice(gp * LANES, (gp + 1) * LANES)
        w = jnp.concatenate([wsgu_ref[2 * gp], wsgu_ref[2 * gp + 1]], axis=1)
        w = jnp.where(causal, w, 0.0).astype(BF16)
        bias = bsp_ref[:, cols]
        rhs = jnp.concatenate(
            [jnp.concatenate([jnp.where(first, vn[rows, cols], zero),
                              jnp.where(first, zero, vn[rows, cols])], axis=0)
             for rows in chunks], axis=1)
        mixed = _dot(w, rhs)
        for c, rows in enumerate(chunks):
            sgu_ref[rows, cols] = (
                gu[rows, cols] * (mixed[:, c * LANES:(c + 1) * LANES] + bias)).astype(BF16)

    v_ref[:, ATTN_WIDTH // 2:] = _dot(h, win_ref[:, v_mid:QKV_WIDTH]).astype(BF16)


def _mix_in(layer, x, mod, g_mix, w_in, g_sgu, w_sgu_s, b_sp):
    b, s, d = x.shape
    tm = TOKEN_TILE
    blocks_per_tile = tm // MOBA_BLOCK
    act = lambda width: jax.ShapeDtypeStruct((b, s, width), BF16)
    act_spec = lambda width: pl.BlockSpec((None, tm, width), lambda bi, i: (bi, i, 0))
    return pl.pallas_call(
        _mix_in_kernel,
        grid=(b, s // tm),
        in_specs=[
            pl.BlockSpec((None, tm, d), lambda bi, i: (bi, i, 0)),
            pl.BlockSpec((None, None, N_MOD, d), lambda bi, i: (layer, bi, 0, 0)),
            _layer_spec(g_mix.shape, layer),
            pl.BlockSpec((d, QKV_WIDTH + UV_WIDTH), lambda bi, i: (0, 0)),
            _layer_spec(g_sgu.shape, layer),
            _layer_spec(w_sgu_s.shape, layer),
            _layer_spec(b_sp.shape, layer),
        ],
        out_specs=[
            act_spec(ATTN_WIDTH), act_spec(ATTN_WIDTH), act_spec(ATTN_WIDTH),
            pl.BlockSpec((None, blocks_per_tile, 1, ATTN_WIDTH), lambda bi, i: (bi, i, 0, 0)),
            act_spec(SGU_WIDTH),
        ],
        out_shape=[
            act(ATTN_WIDTH), act(ATTN_WIDTH), act(ATTN_WIDTH),
            jax.ShapeDtypeStruct((b, s // MOBA_BLOCK, 1, ATTN_WIDTH), F32),
            act(SGU_WIDTH),
        ],
        compiler_params=pltpu.CompilerParams(
            dimension_semantics=("arbitrary", "arbitrary"), vmem_limit_bytes=VMEM_LIMIT),
        name="mix_in",
    )(x, mod, g_mix, w_in, g_sgu, w_sgu_s, b_sp)


def _moba_kernel(*refs, n_cast):
    (q_ref, k_ref, v_ref, km_ref), refs = refs[:4], refs[4:]
    cast_src, refs = refs[:n_cast], refs[n_cast:]
    o_ref, refs = refs[0], refs[1:]
    cast_dst, (vt_ref, ot_ref) = refs[:n_cast], refs[n_cast:]
    _run_cast_job(cast_src, cast_dst)

    nb = km_ref.shape[0]
    s_len = q_ref.shape[0]
    neg_inf = jnp.float32(-jnp.inf)
    blocks = [slice(j * MOBA_BLOCK, (j + 1) * MOBA_BLOCK) for j in range(nb)]

    q = q_ref[...]
    lane = lax.broadcasted_iota(jnp.int32, (s_len, LANES), 1)
    qh = []
    for h in range(HEADS_PER_STEP):
        in_head = (lane >= h * HEAD_DIM) & (lane < (h + 1) * HEAD_DIM)
        qh.append(jnp.where(in_head, q, jnp.zeros((), BF16)))

    def scores(h, i):
        return _dot_nt(k_ref[:(i + 1) * MOBA_BLOCK, :], qh[h][blocks[i]])

    order = [(h, i) for i in range(nb) for h in range(HEADS_PER_STEP)]
    pending = [scores(*order[n]) for n in range(SCORE_LOOKAHEAD)]

    for j in range(nb):
        vt = v_ref[blocks[j], :].astype(F32).T.astype(BF16)
        for h in range(HEADS_PER_STEP):
            vt_ref[h, :HEAD_DIM, blocks[j]] = vt[h * HEAD_DIM:(h + 1) * HEAD_DIM]
    for h in range(HEADS_PER_STEP):
        vt_ref[h, HEAD_DIM:, :] = jnp.ones((BF16_ROWS, s_len), BF16)

    km = km_ref[...]
    km_hi = km.astype(BF16).astype(F32)
    km_lane = lax.broadcasted_iota(jnp.int32, (nb, LANES), 1)
    km_parts = []
    for h in range(HEADS_PER_STEP):
        in_head = (km_lane >= h * HEAD_DIM) & (km_lane < (h + 1) * HEAD_DIM)
        km_parts += [jnp.where(in_head, km_hi, 0.0), jnp.where(in_head, km - km_hi, 0.0)]
    route_parts = _dot_nt(jnp.concatenate(km_parts, axis=0).astype(BF16), q)
    blk = lax.broadcasted_iota(jnp.int32, (nb, s_len), 0)
    q_blk = lax.broadcasted_iota(jnp.int32, (nb, s_len), 1) // MOBA_BLOCK
    kpos = lax.broadcasted_iota(jnp.int32, (MOBA_BLOCK, MOBA_BLOCK), 0)
    qpos = lax.broadcasted_iota(jnp.int32, (MOBA_BLOCK, MOBA_BLOCK), 1)
    causal = kpos <= qpos

    bias = []
    for h in range(HEADS_PER_STEP):
        route = (route_parts[2 * h * nb:(2 * h + 1) * nb]
                 + route_parts[(2 * h + 1) * nb:(2 * h + 2) * nb])
        rank = jnp.zeros((nb, s_len), jnp.int32)
        for jp in range(nb):
            rj = route[jp:jp + 1, :]
            beats = (rj > route) | ((rj == route) & (blk > jp))
            rank = rank + jnp.where(beats & (q_blk > jp), 1, 0)
        bias.append(jnp.where((blk < q_blk) & (rank < MOBA_TOPK), 0.0, neg_inf))

    for n, (h, i) in enumerate(order):
        s = pending.pop(0)
        if n + SCORE_LOOKAHEAD < len(order):
            pending.append(scores(*order[n + SCORE_LOOKAHEAD]))
        parts = [s[blocks[j]] + bias[h][j:j + 1, blocks[i]] for j in range(i)]
        parts.append(jnp.where(causal, s[blocks[i]], neg_inf))
        s = jnp.concatenate(parts, axis=0)
        m = jnp.max(s, axis=0, keepdims=True)
        p = jnp.exp2(s - m).astype(BF16)
        pv = _dot(vt_ref[h, :, :(i + 1) * MOBA_BLOCK], p)
        ot_ref[h * HEAD_DIM:(h + 1) * HEAD_DIM, blocks[i]] = (
            pv[:HEAD_DIM] / pv[HEAD_DIM:HEAD_DIM + 1])
        if h == HEADS_PER_STEP - 1:
            o_ref[blocks[i], :] = ot_ref[:, blocks[i]].T.astype(o_ref.dtype)


def _moba(layer, q, k, v, kmean, cast_weights):
    b, s, width = q.shape
    nb = s // MOBA_BLOCK
    grid = (b, width // LANES)
    slab = pl.BlockSpec((None, s, LANES), lambda bi, p: (bi, 0, p))
    cast_in, cast_out, cast_shapes = _cast_job(cast_weights, layer, grid)
    outs = pl.pallas_call(
        functools.partial(_moba_kernel, n_cast=len(cast_weights)),
        grid=grid,
        in_specs=[slab, slab, slab,
                  pl.BlockSpec((None, nb, LANES), lambda bi, p: (bi, 0, p))] + cast_in,
        out_specs=[slab] + cast_out,
        out_shape=[jax.ShapeDtypeStruct((b, s, width), BF16)] + cast_shapes,
        scratch_shapes=[
            pltpu.VMEM((HEADS_PER_STEP, HEAD_DIM + BF16_ROWS, s), BF16),
            pltpu.VMEM((LANES, s), F32),
        ],
        compiler_params=pltpu.CompilerParams(
            dimension_semantics=("arbitrary", "arbitrary"), vmem_limit_bytes=VMEM_LIMIT),
        name="moba",
    )(q, k, v, kmean, *cast_weights)
    return outs[0], tuple(outs[1:])


def _mix_out_kernel(x_ref, mod_ref, g_ref, attn_ref, sgu_ref, win_ref, wa_ref, ws_ref, wo_ref,
                    o_ref):
    d = x_ref.shape[1]
    gate_cols = QKV_WIDTH + UV_WIDTH
    ya = _dot(attn_ref[...], wa_ref[...])
    ys = _dot(sgu_ref[...], ws_ref[...])
    x = x_ref[...]
    h = _norm_mod(x, g_ref[...], mod_ref[0:1, :], mod_ref[1:2, :]).astype(BF16)
    gates = _dot(h, win_ref[:, gate_cols:])
    merged = _sigmoid(gates[:, :d]) * ya + _sigmoid(gates[:, d:]) * ys
    o_ref[...] = x + mod_ref[2:3, :] * _dot(merged.astype(BF16), wo_ref[...])


def _mix_out(layer, x, mod, g_mix, attn, sgu, w_in, w_attn_br, w_sgu_br, w_out):
    b, s, d = x.shape
    tm = TOKEN_TILE
    tile = lambda width: pl.BlockSpec((None, tm, width), lambda bi, i: (bi, i, 0))
    return pl.pallas_call(
        _mix_out_kernel,
        grid=(b, s // tm),
        in_specs=[
            tile(d),
            pl.BlockSpec((None, None, N_MOD, d), lambda bi, i: (layer, bi, 0, 0)),
            _layer_spec(g_mix.shape, layer),
            tile(attn.shape[2]),
            tile(sgu.shape[2]),
            _full_spec(w_in.shape),
            _full_spec(w_attn_br.shape),
            _full_spec(w_sgu_br.shape),
            _full_spec(w_out.shape),
        ],
        out_specs=tile(d),
        out_shape=jax.ShapeDtypeStruct(x.shape, F32),
        compiler_params=pltpu.CompilerParams(
            dimension_semantics=("arbitrary", "arbitrary"), vmem_limit_bytes=VMEM_LIMIT),
        name="mix_out",
    )(x, mod, g_mix, attn, sgu, w_in, w_attn_br, w_sgu_br, w_out)


def _conv_ffn_kernel(*refs, final_norm, n_cast):
    (x_ref, mod_ref, g_ref, wup_ref, taps_ref, wdown_ref, gfin_ref), refs = refs[:7], refs[7:]
    cast_src, refs = refs[:n_cast], refs[n_cast:]
    o_ref, refs = refs[0], refs[1:]
    cast_dst, refs = refs[:n_cast], refs[n_cast:]
    h_ref, gated_ref, tail_ref, wina_ref, winl_ref = refs

    i = pl.program_id(1)
    tm = x_ref.shape[0]
    d_ff = wdown_ref.shape[0]
    tf = FF_CHUNK
    x = x_ref[...]
    h_ref[...] = _norm_mod(x, g_ref[...], mod_ref[3:4, :], mod_ref[4:5, :]).astype(BF16)

    _run_cast_job(cast_src, cast_dst)

    def conv_chunk(cols, win_ref):
        up = _dot(h_ref[...], wup_ref[:, cols])
        win_ref[:SUBLANES, :] = jnp.where(i > 0, tail_ref[:, cols], 0.0)
        win_ref[SUBLANES:, :] = up
        tail_ref[:, cols] = up[tm - SUBLANES:, :]
        taps = taps_ref[:, cols]
        conv = taps[3:4, :] + taps[2:3, :] * up
        for back in (1, 2):
            conv = conv + taps[2 - back:3 - back, :] * win_ref[pl.ds(SUBLANES - back, tm), :]
        return conv

    for j in range(d_ff // tf):
        act = conv_chunk(slice(j * tf, (j + 1) * tf), wina_ref)
        lin = conv_chunk(slice(d_ff + j * tf, d_ff + (j + 1) * tf), winl_ref)
        gated_ref[:, j * tf:(j + 1) * tf] = (_silu(act) * lin).astype(BF16)

    out = x + mod_ref[5:6, :] * _dot(gated_ref[...], wdown_ref[...])
    if final_norm:
        out = _rms(out, gfin_ref[...])
    o_ref[...] = out


def _conv_ffn(layer, x, mod, g_ffn, w_up, taps, w_down, g_final, final_norm, next_weights):
    b, s, d = x.shape
    tm = TOKEN_TILE
    grid = (b, s // tm)
    d_ff = w_down.shape[0]
    tile = pl.BlockSpec((None, tm, d), lambda bi, i: (bi, i, 0))
    cast_in, cast_out, cast_shapes = _cast_job(next_weights, layer + 1, grid)
    outs = pl.pallas_call(
        functools.partial(_conv_ffn_kernel, final_norm=final_norm, n_cast=len(next_weights)),
        grid=grid,
        in_specs=[
            tile,
            pl.BlockSpec((None, None, N_MOD, d), lambda bi, i: (layer, bi, 0, 0)),
            _layer_spec(g_ffn.shape, layer),
            _full_spec(w_up.shape),
            _layer_spec(taps.shape, layer),
            _full_spec(w_down.shape),
            _full_spec(g_final.shape),
        ] + cast_in,
        out_specs=[tile] + cast_out,
        out_shape=[jax.ShapeDtypeStruct(x.shape, F32)] + cast_shapes,
        scratch_shapes=[
            pltpu.VMEM((tm, d), BF16),
            pltpu.VMEM((tm, d_ff), BF16),
            pltpu.VMEM((SUBLANES, 2 * d_ff), F32),
            pltpu.VMEM((SUBLANES + tm, FF_CHUNK), F32),
            pltpu.VMEM((SUBLANES + tm, FF_CHUNK), F32),
        ],
        compiler_params=pltpu.CompilerParams(
            dimension_semantics=("arbitrary", "arbitrary"), vmem_limit_bytes=VMEM_LIMIT),
        name="conv_ffn",
    )(x, mod, g_ffn, w_up, taps, w_down, g_final, *next_weights)
    return outs[0], tuple(outs[1:])


def kernel(x, c, w_mod, b_mod, g_mix, w_in, g_sgu, w_sgu_s, b_sgu_s, w_attn_br, w_sgu_br, w_out,
           g_ffn, w_up, w_conv, b_conv, w_down, g_final):
    b, s, d = x.shape
    depth = w_mod.shape[0]
    d_ff = w_down.shape[1]
    assert s % TOKEN_TILE == 0 and TOKEN_TILE % MOBA_BLOCK == 0 and d_ff % FF_CHUNK == 0

    later_weights = (w_attn_br, w_sgu_br, w_out, w_up, w_down)
    w_in_b = w_in[0].astype(BF16)
    b_sp = jnp.repeat(jnp.swapaxes(b_sgu_s, 1, 2), SGU_GROUP_DIM, axis=2)
    taps = jnp.concatenate(
        [w_conv, b_conv[:, None, :],
         jnp.zeros((depth, SUBLANES - CONV_WIDTH - 1, 2 * d_ff), F32)], axis=1)
    g_mix3, g_sgu3, g_ffn3 = (g.reshape(depth, 1, -1) for g in (g_mix, g_sgu, g_ffn))
    g_fin = g_final.reshape(1, d)

    mod = _modulation(c, w_mod, b_mod).reshape(depth, b, N_MOD, d)
    for l in range(depth):
        last = l == depth - 1
        q, k, v, kmean, sgu = _mix_in(l, x, mod, g_mix3, w_in_b, g_sgu3, w_sgu_s, b_sp)
        attn, (w_attn_b, w_sgu_b, w_out_b, w_up_b, w_down_b) = _moba(
            l, q, k, v, kmean.reshape(b, s // MOBA_BLOCK, ATTN_WIDTH), later_weights)
        x = _mix_out(l, x, mod, g_mix3, attn, sgu, w_in_b, w_attn_b, w_sgu_b, w_out_b)
        x, next_w_in = _conv_ffn(l, x, mod, g_ffn3, w_up_b, taps, w_down_b, g_fin,
                                 final_norm=last, next_weights=() if last else (w_in,))
        if not last:
            (w_in_b,) = next_w_in
    return x
```

```python
import functools
import math

import jax
import jax.numpy as jnp
from jax import lax
from jax.experimental import pallas as pl
from jax.experimental.pallas import tpu as pltpu

F32 = jnp.float32
BF16 = jnp.bfloat16

N_HEADS = 8
HEAD_DIM = 64
ATTN_WIDTH = N_HEADS * HEAD_DIM
MOBA_BLOCK = 256
MOBA_TOPK = 3
SGU_GROUPS = 8
SGU_GROUP_DIM = 64
SGU_WIDTH = SGU_GROUPS * SGU_GROUP_DIM
SGU_CHUNK = 128
CONV_WIDTH = 3
N_MOD = 6
EPS = 1e-6
QKV_WIDTH = 3 * ATTN_WIDTH
UV_WIDTH = 2 * SGU_WIDTH

LANES = 128
SUBLANES = 8
BF16_ROWS = 16
HEADS_PER_STEP = LANES // HEAD_DIM
VMEM_LIMIT = 56 * 1024 * 1024

TOKEN_TILE = 1024
FF_CHUNK = 256
MOD_COLS = 2048
SCORE_LOOKAHEAD = 3
Q_SCALE = HEAD_DIM ** -0.5 * math.log2(math.e)


def _dot(a, b):
    return jnp.dot(a, b, preferred_element_type=F32)


def _dot_nt(a, b):
    return lax.dot_general(a, b, (((1,), (1,)), ((), ())), preferred_element_type=F32)


def _sigmoid(x):
    return 0.5 + 0.5 * jnp.tanh(0.5 * x)


def _silu(x):
    half = 0.5 * x
    return half + half * jnp.tanh(half)


def _gelu_tanh(x):
    c = 0.7978845608028654
    return 0.5 * x * (1.0 + jnp.tanh(c * (x + 0.044715 * (x * x * x))))


def _rms(x, g):
    return x * lax.rsqrt(jnp.mean(x * x, axis=-1, keepdims=True) + EPS) * g


def _norm_mod(x, g, shift, scale):
    return _rms(x, g * (1.0 + scale)) + shift


def _layer_spec(shape, layer):
    zeros = (0,) * (len(shape) - 1)
    return pl.BlockSpec((None,) + tuple(shape[1:]), lambda *_: (layer,) + zeros)


def _full_spec(shape):
    zeros = (0,) * len(shape)
    return pl.BlockSpec(tuple(shape), lambda *_: zeros)


def _cast_job(weights, layer, grid):
    n_steps = grid[0] * grid[1]
    in_specs, out_specs, out_shapes = [], [], []
    for w in weights:
        _, rows, cols = w.shape
        least = -(-rows // (n_steps * BF16_ROWS)) * BF16_ROWS
        share = next(r for r in range(least, rows + 1, BF16_ROWS) if rows % r == 0)

        def block(g0, g1, last=rows // share - 1):
            return jnp.minimum(g0 * grid[1] + g1, last)

        in_specs.append(pl.BlockSpec(
            (None, share, cols), lambda g0, g1, block=block: (layer, block(g0, g1), 0)))
        out_specs.append(pl.BlockSpec(
            (share, cols), lambda g0, g1, block=block: (block(g0, g1), 0)))
        out_shapes.append(jax.ShapeDtypeStruct((rows, cols), BF16))
    return in_specs, out_specs, out_shapes


def _run_cast_job(src_refs, dst_refs):
    for src, dst in zip(src_refs, dst_refs):
        dst[...] = src[...].astype(BF16)


def _mod_kernel(c_ref, w_ref, b_ref, o_ref):
    c = c_ref[...]
    c_act = _silu(c).astype(BF16)
    o_ref[...] = _dot(c_act, w_ref[...].astype(BF16)) + b_ref[pl.ds(pl.program_id(0), 1), :]


def _modulation(c, w_mod, b_mod):
    depth, d, n = w_mod.shape
    b = c.shape[0]
    return pl.pallas_call(
        _mod_kernel,
        grid=(depth, n // MOD_COLS),
        in_specs=[
            pl.BlockSpec((b, d), lambda l, j: (0, 0)),
            pl.BlockSpec((None, d, MOD_COLS), lambda l, j: (l, 0, j)),
            pl.BlockSpec((depth, MOD_COLS), lambda l, j: (0, j)),
        ],
        out_specs=pl.BlockSpec((None, b, MOD_COLS), lambda l, j: (l, 0, j)),
        out_shape=jax.ShapeDtypeStruct((depth, b, n), F32),
        compiler_params=pltpu.CompilerParams(
            dimension_semantics=("arbitrary", "arbitrary"), vmem_limit_bytes=VMEM_LIMIT),
        name="modulation",
    )(c, w_mod, b_mod)


def _mix_in_kernel(x_ref, mod_ref, g_ref, win_ref, gsgu_ref, wsgu_ref, bsp_ref,
                   q_ref, k_ref, v_ref, kmean_ref, sgu_ref, *, layer):
    tm = x_ref.shape[0]
    g = g_ref[layer:layer + 1, :]
    h = _norm_mod(x_ref[...], g, mod_ref[0:1, :], mod_ref[1:2, :]).astype(BF16)

    uv = _dot(h, win_ref[:, QKV_WIDTH:QKV_WIDTH + UV_WIDTH])
    qk = _dot(h, win_ref[:, :2 * ATTN_WIDTH])
    q_ref[...] = (qk[:, :ATTN_WIDTH] * Q_SCALE).astype(BF16)
    k = qk[:, ATTN_WIDTH:]
    k_ref[...] = k.astype(BF16)
    for c in range(tm // MOBA_BLOCK):
        kmean_ref[c] = jnp.mean(k[c * MOBA_BLOCK:(c + 1) * MOBA_BLOCK], axis=0, keepdims=True)
    v_mid = 2 * ATTN_WIDTH + ATTN_WIDTH // 2
    v_ref[:, :ATTN_WIDTH // 2] = _dot(h, win_ref[:, 2 * ATTN_WIDTH:v_mid]).astype(BF16)

    gu = _gelu_tanh(uv[:, :SGU_WIDTH])
    vn = _rms(_gelu_tanh(uv[:, SGU_WIDTH:]), gsgu_ref[layer:layer + 1, :]).astype(BF16)

    lane = lax.broadcasted_iota(jnp.int32, (SGU_CHUNK, LANES), 1)
    first = lane < SGU_GROUP_DIM
    wrow = lax.broadcasted_iota(jnp.int32, (SGU_CHUNK, 2 * SGU_CHUNK), 0)
    wcol = lax.broadcasted_iota(jnp.int32, (SGU_CHUNK, 2 * SGU_CHUNK), 1)
    causal = jnp.where(wcol >= SGU_CHUNK, wcol - SGU_CHUNK, wcol) <= wrow
    zero = jnp.zeros((), BF16)
    chunks = [slice(c * SGU_CHUNK, (c + 1) * SGU_CHUNK) for c in range(tm // SGU_CHUNK)]
    for gp in range(SGU_WIDTH // LANES):
        cols = slice(gp * LANES, (gp + 1) * LANES)
        w = jnp.concatenate([wsgu_ref[2 * gp], wsgu_ref[2 * gp + 1]], axis=1)
        w = jnp.where(causal, w, 0.0).astype(BF16)
        bias = bsp_ref[:, cols]
        rhs = jnp.concatenate(
            [jnp.concatenate([jnp.where(first, vn[rows, cols], zero),
                              jnp.where(first, zero, vn[rows, cols])], axis=0)
             for rows in chunks], axis=1)
        mixed = _dot(w, rhs)
        for c, rows in enumerate(chunks):
            sgu_ref[rows, cols] = (
                gu[rows, cols] * (mixed[:, c * LANES:(c + 1) * LANES] + bias)).astype(BF16)

    v_ref[:, ATTN_WIDTH // 2:] = _dot(h, win_ref[:, v_mid:QKV_WIDTH]).astype(BF16)


def _mix_in(layer, x, mod, g_mix, w_in, g_sgu, w_sgu_s, b_sp):
    b, s, d = x.shape
    tm = TOKEN_TILE
    blocks_per_tile = tm // MOBA_BLOCK
    act = lambda width: jax.ShapeDtypeStruct((b, s, width), BF16)
    act_spec = lambda width: pl.BlockSpec((None, tm, width), lambda bi, i: (bi, i, 0))
    return pl.pallas_call(
        functools.partial(_mix_in_kernel, layer=layer),
        grid=(b, s // tm),
        in_specs=[
            pl.BlockSpec((None, tm, d), lambda bi, i: (bi, i, 0)),
            pl.BlockSpec((None, None, N_MOD, d), lambda bi, i: (layer, bi, 0, 0)),
            _full_spec(g_mix.shape),
            pl.BlockSpec((d, QKV_WIDTH + UV_WIDTH), lambda bi, i: (0, 0)),
            _full_spec(g_sgu.shape),
            _layer_spec(w_sgu_s.shape, layer),
            _layer_spec(b_sp.shape, layer),
        ],
        out_specs=[
            act_spec(ATTN_WIDTH), act_spec(ATTN_WIDTH), act_spec(ATTN_WIDTH),
            pl.BlockSpec((None, blocks_per_tile, 1, ATTN_WIDTH), lambda bi, i: (bi, i, 0, 0)),
            act_spec(SGU_WIDTH),
        ],
        out_shape=[
            act(ATTN_WIDTH), act(ATTN_WIDTH), act(ATTN_WIDTH),
            jax.ShapeDtypeStruct((b, s // MOBA_BLOCK, 1, ATTN_WIDTH), F32),
            act(SGU_WIDTH),
        ],
        compiler_params=pltpu.CompilerParams(
            dimension_semantics=("arbitrary", "arbitrary"), vmem_limit_bytes=VMEM_LIMIT),
        name="mix_in",
    )(x, mod, g_mix, w_in, g_sgu, w_sgu_s, b_sp)


def _moba_kernel(*refs, n_cast):
    (q_ref, k_ref, v_ref, km_ref), refs = refs[:4], refs[4:]
    cast_src, refs = refs[:n_cast], refs[n_cast:]
    o_ref, refs = refs[0], refs[1:]
    cast_dst, (vt_ref, ot_ref) = refs[:n_cast], refs[n_cast:]
    _run_cast_job(cast_src, cast_dst)

    nb = km_ref.shape[0]
    s_len = q_ref.shape[0]
    neg_inf = jnp.float32(-jnp.inf)
    blocks = [slice(j * MOBA_BLOCK, (j + 1) * MOBA_BLOCK) for j in range(nb)]

    q = q_ref[...]
    lane = lax.broadcasted_iota(jnp.int32, (s_len, LANES), 1)
    qh = []
    for h in range(HEADS_PER_STEP):
        in_head = (lane >= h * HEAD_DIM) & (lane < (h + 1) * HEAD_DIM)
        qh.append(jnp.where(in_head, q, jnp.zeros((), BF16)))

    def scores(h, i):
        return _dot_nt(k_ref[:(i + 1) * MOBA_BLOCK, :], qh[h][blocks[i]])

    order = [(h, i) for i in range(nb) for h in range(HEADS_PER_STEP)]
    pending = [scores(*order[n]) for n in range(SCORE_LOOKAHEAD)]

    for j in range(nb):
        vt = v_ref[blocks[j], :].astype(F32).T.astype(BF16)
        for h in range(HEADS_PER_STEP):
            vt_ref[h, :HEAD_DIM, blocks[j]] = vt[h * HEAD_DIM:(h + 1) * HEAD_DIM]
    for h in range(HEADS_PER_STEP):
        vt_ref[h, HEAD_DIM:, :] = jnp.ones((BF16_ROWS, s_len), BF16)

    km = km_ref[:, 0, :]
    km_hi = km.astype(BF16).astype(F32)
    km_lane = lax.broadcasted_iota(jnp.int32, (nb, LANES), 1)
    km_parts = []
    for h in range(HEADS_PER_STEP):
        in_head = (km_lane >= h * HEAD_DIM) & (km_lane < (h + 1) * HEAD_DIM)
        km_parts += [jnp.where(in_head, km_hi, 0.0), jnp.where(in_head, km - km_hi, 0.0)]
    route_parts = _dot_nt(jnp.concatenate(km_parts, axis=0).astype(BF16), q)
    blk = lax.broadcasted_iota(jnp.int32, (nb, s_len), 0)
    q_blk = lax.broadcasted_iota(jnp.int32, (nb, s_len), 1) // MOBA_BLOCK
    kpos = lax.broadcasted_iota(jnp.int32, (MOBA_BLOCK, MOBA_BLOCK), 0)
    qpos = lax.broadcasted_iota(jnp.int32, (MOBA_BLOCK, MOBA_BLOCK), 1)
    causal = kpos <= qpos

    bias = []
    for h in range(HEADS_PER_STEP):
        route = (route_parts[2 * h * nb:(2 * h + 1) * nb]
                 + route_parts[(2 * h + 1) * nb:(2 * h + 2) * nb])
        rank = jnp.zeros((nb, s_len), jnp.int32)
        for jp in range(nb):
            rj = route[jp:jp + 1, :]
            beats = (rj > route) | ((rj == route) & (blk > jp))
            rank = rank + jnp.where(beats & (q_blk > jp), 1, 0)
        bias.append(jnp.where((blk < q_blk) & (rank < MOBA_TOPK), 0.0, neg_inf))

    for n, (h, i) in enumerate(order):
        s = pending.pop(0)
        if n + SCORE_LOOKAHEAD < len(order):
            pending.append(scores(*order[n + SCORE_LOOKAHEAD]))
        parts = [s[blocks[j]] + bias[h][j:j + 1, blocks[i]] for j in range(i)]
        parts.append(jnp.where(causal, s[blocks[i]], neg_inf))
        s = jnp.concatenate(parts, axis=0)
        m = jnp.max(s, axis=0, keepdims=True)
        p = jnp.exp2(s - m).astype(BF16)
        pv = _dot(vt_ref[h, :, :(i + 1) * MOBA_BLOCK], p)
        ot_ref[h * HEAD_DIM:(h + 1) * HEAD_DIM, blocks[i]] = (
            pv[:HEAD_DIM] / pv[HEAD_DIM:HEAD_DIM + 1])
        if h == HEADS_PER_STEP - 1:
            o_ref[blocks[i], :] = ot_ref[:, blocks[i]].T.astype(o_ref.dtype)


def _moba(layer, q, k, v, kmean, cast_weights):
    b, s, width = q.shape
    nb = s // MOBA_BLOCK
    grid = (b, width // LANES)
    slab = pl.BlockSpec((None, s, LANES), lambda bi, p: (bi, 0, p))
    cast_in, cast_out, cast_shapes = _cast_job(cast_weights, layer, grid)
    outs = pl.pallas_call(
        functools.partial(_moba_kernel, n_cast=len(cast_weights)),
        grid=grid,
        in_specs=[slab, slab, slab,
                  pl.BlockSpec((None, nb, 1, LANES), lambda bi, p: (bi, 0, 0, p))] + cast_in,
        out_specs=[slab] + cast_out,
        out_shape=[jax.ShapeDtypeStruct((b, s, width), BF16)] + cast_shapes,
        scratch_shapes=[
            pltpu.VMEM((HEADS_PER_STEP, HEAD_DIM + BF16_ROWS, s), BF16),
            pltpu.VMEM((LANES, s), F32),
        ],
        compiler_params=pltpu.CompilerParams(
            dimension_semantics=("arbitrary", "arbitrary"), vmem_limit_bytes=VMEM_LIMIT),
        name="moba",
    )(q, k, v, kmean, *cast_weights)
    return outs[0], tuple(outs[1:])


def _mix_out_kernel(x_ref, mod_ref, g_ref, attn_ref, sgu_ref, win_ref, wa_ref, ws_ref, wo_ref,
                    o_ref, *, layer):
    d = x_ref.shape[1]
    gate_cols = QKV_WIDTH + UV_WIDTH
    ya = _dot(attn_ref[...], wa_ref[...])
    ys = _dot(sgu_ref[...], ws_ref[...])
    x = x_ref[...]
    h = _norm_mod(x, g_ref[layer:layer + 1, :], mod_ref[0:1, :], mod_ref[1:2, :]).astype(BF16)
    gates = _dot(h, win_ref[:, gate_cols:])
    merged = _sigmoid(gates[:, :d]) * ya + _sigmoid(gates[:, d:]) * ys
    o_ref[...] = x + mod_ref[2:3, :] * _dot(merged.astype(BF16), wo_ref[...])


def _mix_out(layer, x, mod, g_mix, attn, sgu, w_in, w_attn_br, w_sgu_br, w_out):
    b, s, d = x.shape
    tm = TOKEN_TILE
    tile = lambda width: pl.BlockSpec((None, tm, width), lambda bi, i: (bi, i, 0))
    return pl.pallas_call(
        functools.partial(_mix_out_kernel, layer=layer),
        grid=(b, s // tm),
        in_specs=[
            tile(d),
            pl.BlockSpec((None, None, N_MOD, d), lambda bi, i: (layer, bi, 0, 0)),
            _full_spec(g_mix.shape),
            tile(attn.shape[2]),
            tile(sgu.shape[2]),
            _full_spec(w_in.shape),
            _full_spec(w_attn_br.shape),
            _full_spec(w_sgu_br.shape),
            _full_spec(w_out.shape),
        ],
        out_specs=tile(d),
        out_shape=jax.ShapeDtypeStruct(x.shape, F32),
        compiler_params=pltpu.CompilerParams(
            dimension_semantics=("arbitrary", "arbitrary"), vmem_limit_bytes=VMEM_LIMIT),
        name="mix_out",
    )(x, mod, g_mix, attn, sgu, w_in, w_attn_br, w_sgu_br, w_out)


def _conv_ffn_kernel(*refs, layer, final_norm, n_cast):
    (x_ref, mod_ref, g_ref, wup_ref, taps_ref, wdown_ref, gfin_ref), refs = refs[:7], refs[7:]
    cast_src, refs = refs[:n_cast], refs[n_cast:]
    o_ref, refs = refs[0], refs[1:]
    cast_dst, refs = refs[:n_cast], refs[n_cast:]
    h_ref, gated_ref, tail_ref, wina_ref, winl_ref = refs

    i = pl.program_id(1)
    tm = x_ref.shape[0]
    d_ff = wdown_ref.shape[0]
    tf = FF_CHUNK
    x = x_ref[...]
    g = g_ref[layer:layer + 1, :]
    h_ref[...] = _norm_mod(x, g, mod_ref[3:4, :], mod_ref[4:5, :]).astype(BF16)

    _run_cast_job(cast_src, cast_dst)

    def conv_chunk(cols, win_ref):
        up = _dot(h_ref[...], wup_ref[:, cols])
        win_ref[:SUBLANES, :] = jnp.where(i > 0, tail_ref[:, cols], 0.0)
        win_ref[SUBLANES:, :] = up
        tail_ref[:, cols] = up[tm - SUBLANES:, :]
        taps = taps_ref[:, cols]
        conv = taps[3:4, :] + taps[2:3, :] * up
        for back in (1, 2):
            conv = conv + taps[2 - back:3 - back, :] * win_ref[pl.ds(SUBLANES - back, tm), :]
        return conv

    for j in range(d_ff // tf):
        act = conv_chunk(slice(j * tf, (j + 1) * tf), wina_ref)
        lin = conv_chunk(slice(d_ff + j * tf, d_ff + (j + 1) * tf), winl_ref)
        gated_ref[:, j * tf:(j + 1) * tf] = ((act + act * jnp.tanh(act)) * lin).astype(BF16)

    out = x + mod_ref[5:6, :] * _dot(gated_ref[...], wdown_ref[...])
    if final_norm:
        out = _rms(out, gfin_ref[...])
    o_ref[...] = out


def _conv_ffn(layer, x, mod, g_ffn, w_up, taps, w_down, g_final, final_norm, next_weights):
    b, s, d = x.shape
    tm = TOKEN_TILE
    grid = (b, s // tm)
    d_ff = w_down.shape[0]
    tile = pl.BlockSpec((None, tm, d), lambda bi, i: (bi, i, 0))
    cast_in, cast_out, cast_shapes = _cast_job(next_weights, layer + 1, grid)
    outs = pl.pallas_call(
        functools.partial(_conv_ffn_kernel, layer=layer, final_norm=final_norm,
                          n_cast=len(next_weights)),
        grid=grid,
        in_specs=[
            tile,
            pl.BlockSpec((None, None, N_MOD, d), lambda bi, i: (layer, bi, 0, 0)),
            _full_spec(g_ffn.shape),
            _full_spec(w_up.shape),
            _layer_spec(taps.shape, layer),
            _full_spec(w_down.shape),
            _full_spec(g_final.shape),
        ] + cast_in,
        out_specs=[tile] + cast_out,
        out_shape=[jax.ShapeDtypeStruct(x.shape, F32)] + cast_shapes,
        scratch_shapes=[
            pltpu.VMEM((tm, d), BF16),
            pltpu.VMEM((tm, d_ff), BF16),
            pltpu.VMEM((SUBLANES, 2 * d_ff), F32),
            pltpu.VMEM((SUBLANES + tm, FF_CHUNK), F32),
            pltpu.VMEM((SUBLANES + tm, FF_CHUNK), F32),
        ],
        compiler_params=pltpu.CompilerParams(
            dimension_semantics=("arbitrary", "arbitrary"), vmem_limit_bytes=VMEM_LIMIT),
        name="conv_ffn",
    )(x, mod, g_ffn, w_up, taps, w_down, g_final, *next_weights)
    return outs[0], tuple(outs[1:])


def kernel(x, c, w_mod, b_mod, g_mix, w_in, g_sgu, w_sgu_s, b_sgu_s, w_attn_br, w_sgu_br, w_out,
           g_ffn, w_up, w_conv, b_conv, w_down, g_final):
    b, s, d = x.shape
    depth = w_mod.shape[0]
    d_ff = w_down.shape[1]
    assert s % TOKEN_TILE == 0 and TOKEN_TILE % MOBA_BLOCK == 0 and d_ff % FF_CHUNK == 0

    later_weights = (w_attn_br, w_sgu_br, w_out, w_up, w_down)
    w_in_b = w_in[0].astype(BF16)
    b_sp = jnp.repeat(jnp.swapaxes(b_sgu_s, 1, 2), SGU_GROUP_DIM, axis=2)
    taps = jnp.concatenate(
        [w_conv, b_conv[:, None, :],
         jnp.zeros((depth, SUBLANES - CONV_WIDTH - 1, 2 * d_ff), F32)], axis=1)
    taps = taps * jnp.where(jnp.arange(2 * d_ff) < d_ff, 0.5, 1.0).astype(F32)
    g_fin = g_final.reshape(1, d)

    mod = _modulation(c, w_mod, b_mod).reshape(depth, b, N_MOD, d)
    for l in range(depth):
        last = l == depth - 1
        q, k, v, kmean, sgu = _mix_in(l, x, mod, g_mix, w_in_b, g_sgu, w_sgu_s, b_sp)
        attn, (w_attn_b, w_sgu_b, w_out_b, w_up_b, w_down_b) = _moba(
            l, q, k, v, kmean, later_weights)
        x = _mix_out(l, x, mod, g_mix, attn, sgu, w_in_b, w_attn_b, w_sgu_b, w_out_b)
        x, next_w_in = _conv_ffn(l, x, mod, g_ffn, w_up_b, taps, w_down_b, g_fin,
                                 final_norm=last, next_weights=() if last else (w_in,))
        if not last:
            (w_in_b,) = next_w_in
    return x
```

```python
import functools
import math

import jax
import jax.numpy as jnp
from jax import lax
from jax.experimental import pallas as pl
from jax.experimental.pallas import tpu as pltpu

F32 = jnp.float32
BF16 = jnp.bfloat16

N_HEADS = 8
HEAD_DIM = 64
ATTN_WIDTH = N_HEADS * HEAD_DIM
MOBA_BLOCK = 256
MOBA_TOPK = 3
SGU_GROUPS = 8
SGU_GROUP_DIM = 64
SGU_WIDTH = SGU_GROUPS * SGU_GROUP_DIM
SGU_CHUNK = 128
CONV_WIDTH = 3
N_MOD = 6
EPS = 1e-6
QKV_WIDTH = 3 * ATTN_WIDTH
UV_WIDTH = 2 * SGU_WIDTH

LANES = 128
SUBLANES = 8
BF16_ROWS = 16
HEADS_PER_STEP = LANES // HEAD_DIM
VMEM_LIMIT = 56 * 1024 * 1024

TOKEN_TILE = 1024
FUSED_TILE = 512
FF_CHUNK = 256
MOD_COLS = 2048
SCORE_LOOKAHEAD = 3
Q_SCALE = HEAD_DIM ** -0.5 * math.log2(math.e)


def _dot(a, b):
    return jnp.dot(a, b, preferred_element_type=F32)


def _dot_nt(a, b):
    return lax.dot_general(a, b, (((1,), (1,)), ((), ())), preferred_element_type=F32)


def _sigmoid(x):
    return 0.5 + 0.5 * jnp.tanh(0.5 * x)


def _silu(x):
    half = 0.5 * x
    return half + half * jnp.tanh(half)


def _gelu_tanh(x):
    c = 0.7978845608028654
    return 0.5 * x * (1.0 + jnp.tanh(c * (x + 0.044715 * (x * x * x))))


def _rms(x, g):
    return x * lax.rsqrt(jnp.mean(x * x, axis=-1, keepdims=True) + EPS) * g


def _norm_mod(x, g, shift, scale):
    return _rms(x, g * (1.0 + scale)) + shift


def _layer_spec(shape, layer):
    zeros = (0,) * (len(shape) - 1)
    return pl.BlockSpec((None,) + tuple(shape[1:]), lambda *_: (layer,) + zeros)


def _full_spec(shape):
    zeros = (0,) * len(shape)
    return pl.BlockSpec(tuple(shape), lambda *_: zeros)


def _cast_job(weights, layer, grid):
    n_steps = grid[0] * grid[1]
    in_specs, out_specs, out_shapes = [], [], []
    for w in weights:
        _, rows, cols = w.shape
        least = -(-rows // (n_steps * BF16_ROWS)) * BF16_ROWS
        share = next(r for r in range(least, rows + 1, BF16_ROWS) if rows % r == 0)

        def block(g0, g1, last=rows // share - 1):
            return jnp.minimum(g0 * grid[1] + g1, last)

        in_specs.append(pl.BlockSpec(
            (None, share, cols), lambda g0, g1, block=block: (layer, block(g0, g1), 0)))
        out_specs.append(pl.BlockSpec(
            (share, cols), lambda g0, g1, block=block: (block(g0, g1), 0)))
        out_shapes.append(jax.ShapeDtypeStruct((rows, cols), BF16))
    return in_specs, out_specs, out_shapes


def _run_cast_job(src_refs, dst_refs):
    for src, dst in zip(src_refs, dst_refs):
        dst[...] = src[...].astype(BF16)


def _mod_kernel(c_ref, w_ref, b_ref, o_ref):
    c = c_ref[...]
    c_act = _silu(c).astype(BF16)
    o_ref[...] = _dot(c_act, w_ref[...].astype(BF16)) + b_ref[pl.ds(pl.program_id(0), 1), :]


def _modulation(c, w_mod, b_mod):
    depth, d, n = w_mod.shape
    b = c.shape[0]
    return pl.pallas_call(
        _mod_kernel,
        grid=(depth, n // MOD_COLS),
        in_specs=[
            pl.BlockSpec((b, d), lambda l, j: (0, 0)),
            pl.BlockSpec((None, d, MOD_COLS), lambda l, j: (l, 0, j)),
            pl.BlockSpec((depth, MOD_COLS), lambda l, j: (0, j)),
        ],
        out_specs=pl.BlockSpec((None, b, MOD_COLS), lambda l, j: (l, 0, j)),
        out_shape=jax.ShapeDtypeStruct((depth, b, n), F32),
        compiler_params=pltpu.CompilerParams(
            dimension_semantics=("arbitrary", "arbitrary"), vmem_limit_bytes=VMEM_LIMIT),
        name="modulation",
    )(c, w_mod, b_mod)


def _mix_in_kernel(x_ref, mod_ref, g_ref, win_ref, gsgu_ref, wsgu_ref, bsp_ref,
                   q_ref, k_ref, v_ref, kmean_ref, sgu_ref, *, layer):
    tm = x_ref.shape[0]
    g = g_ref[layer:layer + 1, :]
    h = _norm_mod(x_ref[...], g, mod_ref[0:1, :], mod_ref[1:2, :]).astype(BF16)

    uv = _dot(h, win_ref[:, QKV_WIDTH:QKV_WIDTH + UV_WIDTH])
    qk = _dot(h, win_ref[:, :2 * ATTN_WIDTH])
    q_ref[...] = (qk[:, :ATTN_WIDTH] * Q_SCALE).astype(BF16)
    k = qk[:, ATTN_WIDTH:]
    k_ref[...] = k.astype(BF16)
    for c in range(tm // MOBA_BLOCK):
        kmean_ref[c] = jnp.mean(k[c * MOBA_BLOCK:(c + 1) * MOBA_BLOCK], axis=0, keepdims=True)
    v_mid = 2 * ATTN_WIDTH + ATTN_WIDTH // 2
    v_ref[:, :ATTN_WIDTH // 2] = _dot(h, win_ref[:, 2 * ATTN_WIDTH:v_mid]).astype(BF16)

    gu = _gelu_tanh(uv[:, :SGU_WIDTH])
    vn = _rms(_gelu_tanh(uv[:, SGU_WIDTH:]), gsgu_ref[layer:layer + 1, :]).astype(BF16)

    lane = lax.broadcasted_iota(jnp.int32, (SGU_CHUNK, LANES), 1)
    first = lane < SGU_GROUP_DIM
    wrow = lax.broadcasted_iota(jnp.int32, (SGU_CHUNK, 2 * SGU_CHUNK), 0)
    wcol = lax.broadcasted_iota(jnp.int32, (SGU_CHUNK, 2 * SGU_CHUNK), 1)
    causal = jnp.where(wcol >= SGU_CHUNK, wcol - SGU_CHUNK, wcol) <= wrow
    zero = jnp.zeros((), BF16)
    chunks = [slice(c * SGU_CHUNK, (c + 1) * SGU_CHUNK) for c in range(tm // SGU_CHUNK)]
    for gp in range(SGU_WIDTH // LANES):
        cols = slice(gp * LANES, (gp + 1) * LANES)
        w = jnp.concatenate([wsgu_ref[2 * gp], wsgu_ref[2 * gp + 1]], axis=1)
        w = jnp.where(causal, w, 0.0).astype(BF16)
        bias = bsp_ref[:, cols]
        rhs = jnp.concatenate(
            [jnp.concatenate([jnp.where(first, vn[rows, cols], zero),
                              jnp.where(first, zero, vn[rows, cols])], axis=0)
             for rows in chunks], axis=1)
        mixed = _dot(w, rhs)
        for c, rows in enumerate(chunks):
            sgu_ref[rows, cols] = (
                gu[rows, cols] * (mixed[:, c * LANES:(c + 1) * LANES] + bias)).astype(BF16)

    v_ref[:, ATTN_WIDTH // 2:] = _dot(h, win_ref[:, v_mid:QKV_WIDTH]).astype(BF16)


def _mix_in(layer, x, mod, g_mix, w_in, g_sgu, w_sgu_s, b_sp):
    b, s, d = x.shape
    tm = TOKEN_TILE
    blocks_per_tile = tm // MOBA_BLOCK
    act = lambda width: jax.ShapeDtypeStruct((b, s, width), BF16)
    act_spec = lambda width: pl.BlockSpec((None, tm, width), lambda bi, i: (bi, i, 0))
    return pl.pallas_call(
        functools.partial(_mix_in_kernel, layer=layer),
        grid=(b, s // tm),
        in_specs=[
            pl.BlockSpec((None, tm, d), lambda bi, i: (bi, i, 0)),
            pl.BlockSpec((None, None, N_MOD, d), lambda bi, i: (layer, bi, 0, 0)),
            _full_spec(g_mix.shape),
            pl.BlockSpec((d, QKV_WIDTH + UV_WIDTH), lambda bi, i: (0, 0)),
            _full_spec(g_sgu.shape),
            _layer_spec(w_sgu_s.shape, layer),
            _layer_spec(b_sp.shape, layer),
        ],
        out_specs=[
            act_spec(ATTN_WIDTH), act_spec(ATTN_WIDTH), act_spec(ATTN_WIDTH),
            pl.BlockSpec((None, blocks_per_tile, 1, ATTN_WIDTH), lambda bi, i: (bi, i, 0, 0)),
            act_spec(SGU_WIDTH),
        ],
        out_shape=[
            act(ATTN_WIDTH), act(ATTN_WIDTH), act(ATTN_WIDTH),
            jax.ShapeDtypeStruct((b, s // MOBA_BLOCK, 1, ATTN_WIDTH), F32),
            act(SGU_WIDTH),
        ],
        compiler_params=pltpu.CompilerParams(
            dimension_semantics=("arbitrary", "arbitrary"), vmem_limit_bytes=VMEM_LIMIT),
        name="mix_in",
    )(x, mod, g_mix, w_in, g_sgu, w_sgu_s, b_sp)


def _moba_kernel(*refs, n_cast):
    (q_ref, k_ref, v_ref, km_ref), refs = refs[:4], refs[4:]
    cast_src, refs = refs[:n_cast], refs[n_cast:]
    o_ref, refs = refs[0], refs[1:]
    cast_dst, (vt_ref, ot_ref) = refs[:n_cast], refs[n_cast:]
    _run_cast_job(cast_src, cast_dst)

    nb = km_ref.shape[0]
    s_len = q_ref.shape[0]
    neg_inf = jnp.float32(-jnp.inf)
    blocks = [slice(j * MOBA_BLOCK, (j + 1) * MOBA_BLOCK) for j in range(nb)]

    q = q_ref[...]
    lane = lax.broadcasted_iota(jnp.int32, (s_len, LANES), 1)
    qh = []
    for h in range(HEADS_PER_STEP):
        in_head = (lane >= h * HEAD_DIM) & (lane < (h + 1) * HEAD_DIM)
        qh.append(jnp.where(in_head, q, jnp.zeros((), BF16)))

    def scores(h, i):
        return _dot_nt(k_ref[:(i + 1) * MOBA_BLOCK, :], qh[h][blocks[i]])

    order = [(h, i) for i in range(nb) for h in range(HEADS_PER_STEP)]
    pending = [scores(*order[n]) for n in range(SCORE_LOOKAHEAD)]

    for j in range(nb):
        vt = v_ref[blocks[j], :].astype(F32).T.astype(BF16)
        for h in range(HEADS_PER_STEP):
            vt_ref[h, :HEAD_DIM, blocks[j]] = vt[h * HEAD_DIM:(h + 1) * HEAD_DIM]
    for h in range(HEADS_PER_STEP):
        vt_ref[h, HEAD_DIM:, :] = jnp.ones((BF16_ROWS, s_len), BF16)

    km = km_ref[:, 0, :]
    km_hi = km.astype(BF16).astype(F32)
    km_lane = lax.broadcasted_iota(jnp.int32, (nb, LANES), 1)
    km_parts = []
    for h in range(HEADS_PER_STEP):
        in_head = (km_lane >= h * HEAD_DIM) & (km_lane < (h + 1) * HEAD_DIM)
        km_parts += [jnp.where(in_head, km_hi, 0.0), jnp.where(in_head, km - km_hi, 0.0)]
    route_parts = _dot_nt(jnp.concatenate(km_parts, axis=0).astype(BF16), q)
    blk = lax.broadcasted_iota(jnp.int32, (nb, s_len), 0)
    q_blk = lax.broadcasted_iota(jnp.int32, (nb, s_len), 1) // MOBA_BLOCK
    kpos = lax.broadcasted_iota(jnp.int32, (MOBA_BLOCK, MOBA_BLOCK), 0)
    qpos = lax.broadcasted_iota(jnp.int32, (MOBA_BLOCK, MOBA_BLOCK), 1)
    causal = kpos <= qpos

    bias = []
    for h in range(HEADS_PER_STEP):
        route = (route_parts[2 * h * nb:(2 * h + 1) * nb]
                 + route_parts[(2 * h + 1) * nb:(2 * h + 2) * nb])
        rank = jnp.zeros((nb, s_len), jnp.int32)
        for jp in range(nb):
            rj = route[jp:jp + 1, :]
            beats = (rj > route) | ((rj == route) & (blk > jp))
            rank = rank + jnp.where(beats & (q_blk > jp), 1, 0)
        bias.append(jnp.where((blk < q_blk) & (rank < MOBA_TOPK), 0.0, neg_inf))

    for n, (h, i) in enumerate(order):
        s = pending.pop(0)
        if n + SCORE_LOOKAHEAD < len(order):
            pending.append(scores(*order[n + SCORE_LOOKAHEAD]))
        parts = [s[blocks[j]] + bias[h][j:j + 1, blocks[i]] for j in range(i)]
        parts.append(jnp.where(causal, s[blocks[i]], neg_inf))
        s = jnp.concatenate(parts, axis=0)
        m = jnp.max(s, axis=0, keepdims=True)
        p = jnp.exp2(s - m).astype(BF16)
        pv = _dot(vt_ref[h, :, :(i + 1) * MOBA_BLOCK], p)
        ot_ref[h * HEAD_DIM:(h + 1) * HEAD_DIM, blocks[i]] = (
            pv[:HEAD_DIM] / pv[HEAD_DIM:HEAD_DIM + 1])
        if h == HEADS_PER_STEP - 1:
            o_ref[blocks[i], :] = ot_ref[:, blocks[i]].T.astype(o_ref.dtype)


def _moba(layer, q, k, v, kmean, cast_weights):
    b, s, width = q.shape
    nb = s // MOBA_BLOCK
    grid = (b, width // LANES)
    slab = pl.BlockSpec((None, s, LANES), lambda bi, p: (bi, 0, p))
    cast_in, cast_out, cast_shapes = _cast_job(cast_weights, layer, grid)
    outs = pl.pallas_call(
        functools.partial(_moba_kernel, n_cast=len(cast_weights)),
        grid=grid,
        in_specs=[slab, slab, slab,
                  pl.BlockSpec((None, nb, 1, LANES), lambda bi, p: (bi, 0, 0, p))] + cast_in,
        out_specs=[slab] + cast_out,
        out_shape=[jax.ShapeDtypeStruct((b, s, width), BF16)] + cast_shapes,
        scratch_shapes=[
            pltpu.VMEM((HEADS_PER_STEP, HEAD_DIM + BF16_ROWS, s), BF16),
            pltpu.VMEM((LANES, s), F32),
        ],
        compiler_params=pltpu.CompilerParams(
            dimension_semantics=("arbitrary", "arbitrary"), vmem_limit_bytes=VMEM_LIMIT),
        name="moba",
    )(q, k, v, kmean, *cast_weights)
    return outs[0], tuple(outs[1:])


def _mix_out_kernel(x_ref, mod_ref, g_ref, attn_ref, sgu_ref, win_ref, wa_ref, ws_ref, wo_ref,
                    o_ref, *, layer):
    d = x_ref.shape[1]
    gate_cols = QKV_WIDTH + UV_WIDTH
    ya = _dot(attn_ref[...], wa_ref[...])
    ys = _dot(sgu_ref[...], ws_ref[...])
    x = x_ref[...]
    h = _norm_mod(x, g_ref[layer:layer + 1, :], mod_ref[0:1, :], mod_ref[1:2, :]).astype(BF16)
    gates = _dot(h, win_ref[:, gate_cols:])
    merged = _sigmoid(gates[:, :d]) * ya + _sigmoid(gates[:, d:]) * ys
    o_ref[...] = x + mod_ref[2:3, :] * _dot(merged.astype(BF16), wo_ref[...])


def _mix_out(layer, x, mod, g_mix, attn, sgu, w_in, w_attn_br, w_sgu_br, w_out):
    b, s, d = x.shape
    tm = TOKEN_TILE
    tile = lambda width: pl.BlockSpec((None, tm, width), lambda bi, i: (bi, i, 0))
    return pl.pallas_call(
        functools.partial(_mix_out_kernel, layer=layer),
        grid=(b, s // tm),
        in_specs=[
            tile(d),
            pl.BlockSpec((None, None, N_MOD, d), lambda bi, i: (layer, bi, 0, 0)),
            _full_spec(g_mix.shape),
            tile(attn.shape[2]),
            tile(sgu.shape[2]),
            _full_spec(w_in.shape),
            _full_spec(w_attn_br.shape),
            _full_spec(w_sgu_br.shape),
            _full_spec(w_out.shape),
        ],
        out_specs=tile(d),
        out_shape=jax.ShapeDtypeStruct(x.shape, F32),
        compiler_params=pltpu.CompilerParams(
            dimension_semantics=("arbitrary", "arbitrary"), vmem_limit_bytes=VMEM_LIMIT),
        name="mix_out",
    )(x, mod, g_mix, attn, sgu, w_in, w_attn_br, w_sgu_br, w_out)


def _conv_ffn_kernel(*refs, layer, final_norm, n_cast):
    (x_ref, mod_ref, g_ref, wup_ref, taps_ref, wdown_ref, gfin_ref), refs = refs[:7], refs[7:]
    cast_src, refs = refs[:n_cast], refs[n_cast:]
    o_ref, refs = refs[0], refs[1:]
    cast_dst, refs = refs[:n_cast], refs[n_cast:]
    h_ref, gated_ref, tail_ref, wina_ref, winl_ref = refs

    i = pl.program_id(1)
    tm = x_ref.shape[0]
    d_ff = wdown_ref.shape[0]
    tf = FF_CHUNK
    x = x_ref[...]
    g = g_ref[layer:layer + 1, :]
    h_ref[...] = _norm_mod(x, g, mod_ref[3:4, :], mod_ref[4:5, :]).astype(BF16)

    _run_cast_job(cast_src, cast_dst)

    def conv_chunk(cols, win_ref):
        up = _dot(h_ref[...], wup_ref[:, cols])
        win_ref[:SUBLANES, :] = jnp.where(i > 0, tail_ref[:, cols], 0.0)
        win_ref[SUBLANES:, :] = up
        tail_ref[:, cols] = up[tm - SUBLANES:, :]
        taps = taps_ref[:, cols]
        conv = taps[3:4, :] + taps[2:3, :] * up
        for back in (1, 2):
            conv = conv + taps[2 - back:3 - back, :] * win_ref[pl.ds(SUBLANES - back, tm), :]
        return conv

    for j in range(d_ff // tf):
        act = conv_chunk(slice(j * tf, (j + 1) * tf), wina_ref)
        lin = conv_chunk(slice(d_ff + j * tf, d_ff + (j + 1) * tf), winl_ref)
        gated_ref[:, j * tf:(j + 1) * tf] = ((act + act * jnp.tanh(act)) * lin).astype(BF16)

    out = x + mod_ref[5:6, :] * _dot(gated_ref[...], wdown_ref[...])
    if final_norm:
        out = _rms(out, gfin_ref[...])
    o_ref[...] = out


def _conv_ffn(layer, x, mod, g_ffn, w_up, taps, w_down, g_final, final_norm, next_weights):
    b, s, d = x.shape
    tm = TOKEN_TILE
    grid = (b, s // tm)
    d_ff = w_down.shape[0]
    tile = pl.BlockSpec((None, tm, d), lambda bi, i: (bi, i, 0))
    cast_in, cast_out, cast_shapes = _cast_job(next_weights, layer + 1, grid)
    outs = pl.pallas_call(
        functools.partial(_conv_ffn_kernel, layer=layer, final_norm=final_norm,
                          n_cast=len(next_weights)),
        grid=grid,
        in_specs=[
            tile,
            pl.BlockSpec((None, None, N_MOD, d), lambda bi, i: (layer, bi, 0, 0)),
            _full_spec(g_ffn.shape),
            _full_spec(w_up.shape),
            _layer_spec(taps.shape, layer),
            _full_spec(w_down.shape),
            _full_spec(g_final.shape),
        ] + cast_in,
        out_specs=[tile] + cast_out,
        out_shape=[jax.ShapeDtypeStruct(x.shape, F32)] + cast_shapes,
        scratch_shapes=[
            pltpu.VMEM((tm, d), BF16),
            pltpu.VMEM((tm, d_ff), BF16),
            pltpu.VMEM((SUBLANES, 2 * d_ff), F32),
            pltpu.VMEM((SUBLANES + tm, FF_CHUNK), F32),
            pltpu.VMEM((SUBLANES + tm, FF_CHUNK), F32),
        ],
        compiler_params=pltpu.CompilerParams(
            dimension_semantics=("arbitrary", "arbitrary"), vmem_limit_bytes=VMEM_LIMIT),
        name="conv_ffn",
    )(x, mod, g_ffn, w_up, taps, w_down, g_final, *next_weights)
    return outs[0], tuple(outs[1:])


def _mix_ffn_kernel(*refs, layer, final_norm, n_cast):
    (x_ref, mod_ref, gmix_ref, attn_ref, sgu_ref, win_ref, wa_ref, ws_ref, wo_ref,
     gffn_ref, wup_ref, taps_ref, wdown_ref, gfin_ref), refs = refs[:14], refs[14:]
    cast_src, refs = refs[:n_cast], refs[n_cast:]
    o_ref, refs = refs[0], refs[1:]
    cast_dst, refs = refs[:n_cast], refs[n_cast:]
    h_ref, gated_ref, tail_ref, wina_ref, winl_ref = refs

    i = pl.program_id(1)
    tm, d = x_ref.shape
    d_ff = wdown_ref.shape[0]
    tf = FF_CHUNK
    gate_cols = QKV_WIDTH + UV_WIDTH

    ya = _dot(attn_ref[...], wa_ref[...])
    ys = _dot(sgu_ref[...], ws_ref[...])
    x0 = x_ref[...]
    h = _norm_mod(x0, gmix_ref[layer:layer + 1, :], mod_ref[0:1, :], mod_ref[1:2, :]).astype(BF16)
    gates = _dot(h, win_ref[:, gate_cols:])
    merged = _sigmoid(gates[:, :d]) * ya + _sigmoid(gates[:, d:]) * ys
    x = x0 + mod_ref[2:3, :] * _dot(merged.astype(BF16), wo_ref[...])

    g = gffn_ref[layer:layer + 1, :]
    h_ref[...] = _norm_mod(x, g, mod_ref[3:4, :], mod_ref[4:5, :]).astype(BF16)
    _run_cast_job(cast_src, cast_dst)

    def conv_chunk(cols, win_ref):
        up = _dot(h_ref[...], wup_ref[:, cols])
        win_ref[:SUBLANES, :] = jnp.where(i > 0, tail_ref[:, cols], 0.0)
        win_ref[SUBLANES:, :] = up
        tail_ref[:, cols] = up[tm - SUBLANES:, :]
        taps = taps_ref[:, cols]
        conv = taps[3:4, :] + taps[2:3, :] * up
        for back in (1, 2):
            conv = conv + taps[2 - back:3 - back, :] * win_ref[pl.ds(SUBLANES - back, tm), :]
        return conv

    for j in range(d_ff // tf):
        act = conv_chunk(slice(j * tf, (j + 1) * tf), wina_ref)
        lin = conv_chunk(slice(d_ff + j * tf, d_ff + (j + 1) * tf), winl_ref)
        gated_ref[:, j * tf:(j + 1) * tf] = ((act + act * jnp.tanh(act)) * lin).astype(BF16)

    out = x + mod_ref[5:6, :] * _dot(gated_ref[...], wdown_ref[...])
    if final_norm:
        out = _rms(out, gfin_ref[...])
    o_ref[...] = out


def _mix_ffn(layer, x, mod, g_mix, attn, sgu, w_in, w_attn_br, w_sgu_br, w_out, g_ffn, w_up, taps,
             w_down, g_final, final_norm, next_weights):
    b, s, d = x.shape
    tm = FUSED_TILE
    grid = (b, s // tm)
    d_ff = w_down.shape[0]
    tile = lambda width: pl.BlockSpec((None, tm, width), lambda bi, i: (bi, i, 0))
    cast_in, cast_out, cast_shapes = _cast_job(next_weights, layer + 1, grid)
    outs = pl.pallas_call(
        functools.partial(_mix_ffn_kernel, layer=layer, final_norm=final_norm,
                          n_cast=len(next_weights)),
        grid=grid,
        in_specs=[
            tile(d),
            pl.BlockSpec((None, None, N_MOD, d), lambda bi, i: (layer, bi, 0, 0)),
            _full_spec(g_mix.shape),
            tile(attn.shape[2]),
            tile(sgu.shape[2]),
            _full_spec(w_in.shape),
            _full_spec(w_attn_br.shape),
            _full_spec(w_sgu_br.shape),
            _full_spec(w_out.shape),
            _full_spec(g_ffn.shape),
            _full_spec(w_up.shape),
            _layer_spec(taps.shape, layer),
            _full_spec(w_down.shape),
            _full_spec(g_final.shape),
        ] + cast_in,
        out_specs=[tile(d)] + cast_out,
        out_shape=[jax.ShapeDtypeStruct(x.shape, F32)] + cast_shapes,
        scratch_shapes=[
            pltpu.VMEM((tm, d), BF16),
            pltpu.VMEM((tm, d_ff), BF16),
            pltpu.VMEM((SUBLANES, 2 * d_ff), F32),
            pltpu.VMEM((SUBLANES + tm, FF_CHUNK), F32),
            pltpu.VMEM((SUBLANES + tm, FF_CHUNK), F32),
        ],
        compiler_params=pltpu.CompilerParams(
            dimension_semantics=("arbitrary", "arbitrary"), vmem_limit_bytes=VMEM_LIMIT),
        name="mix_ffn",
    )(x, mod, g_mix, attn, sgu, w_in, w_attn_br, w_sgu_br, w_out, g_ffn, w_up, taps, w_down,
      g_final, *next_weights)
    return outs[0], tuple(outs[1:])


def kernel(x, c, w_mod, b_mod, g_mix, w_in, g_sgu, w_sgu_s, b_sgu_s, w_attn_br, w_sgu_br, w_out,
           g_ffn, w_up, w_conv, b_conv, w_down, g_final):
    b, s, d = x.shape
    depth = w_mod.shape[0]
    d_ff = w_down.shape[1]
    assert s % TOKEN_TILE == 0 and TOKEN_TILE % MOBA_BLOCK == 0 and d_ff % FF_CHUNK == 0

    later_weights = (w_attn_br, w_sgu_br, w_out, w_up, w_down)
    w_in_b = w_in[0].astype(BF16)
    b_sp = jnp.repeat(jnp.swapaxes(b_sgu_s, 1, 2), SGU_GROUP_DIM, axis=2)
    taps = jnp.concatenate(
        [w_conv, b_conv[:, None, :],
         jnp.zeros((depth, SUBLANES - CONV_WIDTH - 1, 2 * d_ff), F32)], axis=1)
    taps = taps * jnp.where(jnp.arange(2 * d_ff) < d_ff, 0.5, 1.0).astype(F32)
    g_fin = g_final.reshape(1, d)

    mod = _modulation(c, w_mod, b_mod).reshape(depth, b, N_MOD, d)
    for l in range(depth):
        last = l == depth - 1
        q, k, v, kmean, sgu = _mix_in(l, x, mod, g_mix, w_in_b, g_sgu, w_sgu_s, b_sp)
        attn, (w_attn_b, w_sgu_b, w_out_b, w_up_b, w_down_b) = _moba(
            l, q, k, v, kmean, later_weights)
        x, next_w_in = _mix_ffn(l, x, mod, g_mix, attn, sgu, w_in_b, w_attn_b, w_sgu_b, w_out_b,
                                g_ffn, w_up_b, taps, w_down_b, g_fin,
                                final_norm=last, next_weights=() if last else (w_in,))
        if not last:
            (w_in_b,) = next_w_in
    return x
```

```python
import functools
import math

import jax
import jax.numpy as jnp
from jax import lax
from jax.experimental import pallas as pl
from jax.experimental.pallas import tpu as pltpu

F32 = jnp.float32
BF16 = jnp.bfloat16

N_HEADS = 8
HEAD_DIM = 64
ATTN_WIDTH = N_HEADS * HEAD_DIM
MOBA_BLOCK = 256
MOBA_TOPK = 3
SGU_GROUPS = 8
SGU_GROUP_DIM = 64
SGU_WIDTH = SGU_GROUPS * SGU_GROUP_DIM
SGU_CHUNK = 128
CONV_WIDTH = 3
N_MOD = 6
EPS = 1e-6
QKV_WIDTH = 3 * ATTN_WIDTH
UV_WIDTH = 2 * SGU_WIDTH

LANES = 128
SUBLANES = 8
BF16_ROWS = 16
HEADS_PER_STEP = LANES // HEAD_DIM
VMEM_LIMIT = 56 * 1024 * 1024

TOKEN_TILE = 1024
FUSED_TILE = 512
FF_CHUNK = 256
MOD_COLS = 2048
SCORE_LOOKAHEAD = 3
Q_SCALE = HEAD_DIM ** -0.5 * math.log2(math.e)


def _dot(a, b):
    return jnp.dot(a, b, preferred_element_type=F32)


def _dot_nt(a, b):
    return lax.dot_general(a, b, (((1,), (1,)), ((), ())), preferred_element_type=F32)


def _sigmoid(x):
    return 0.5 + 0.5 * jnp.tanh(0.5 * x)


def _silu(x):
    half = 0.5 * x
    return half + half * jnp.tanh(half)


def _gelu_tanh(x):
    c = 0.7978845608028654
    return 0.5 * x * (1.0 + jnp.tanh(c * (x + 0.044715 * (x * x * x))))


def _rms(x, g):
    return x * lax.rsqrt(jnp.mean(x * x, axis=-1, keepdims=True) + EPS) * g


def _norm_mod(x, g, shift, scale):
    return _rms(x, g * (1.0 + scale)) + shift


def _layer_spec(shape, layer):
    zeros = (0,) * (len(shape) - 1)
    return pl.BlockSpec((None,) + tuple(shape[1:]), lambda *_: (layer,) + zeros)


def _full_spec(shape):
    zeros = (0,) * len(shape)
    return pl.BlockSpec(tuple(shape), lambda *_: zeros)


def _cast_job(weights, layer, grid):
    n_steps = grid[0] * grid[1]
    in_specs, out_specs, out_shapes = [], [], []
    for w in weights:
        _, rows, cols = w.shape
        least = -(-rows // (n_steps * BF16_ROWS)) * BF16_ROWS
        share = next(r for r in range(least, rows + 1, BF16_ROWS) if rows % r == 0)

        def block(g0, g1, last=rows // share - 1):
            return jnp.minimum(g0 * grid[1] + g1, last)

        in_specs.append(pl.BlockSpec(
            (None, share, cols), lambda g0, g1, block=block: (layer, block(g0, g1), 0)))
        out_specs.append(pl.BlockSpec(
            (share, cols), lambda g0, g1, block=block: (block(g0, g1), 0)))
        out_shapes.append(jax.ShapeDtypeStruct((rows, cols), BF16))
    return in_specs, out_specs, out_shapes


def _run_cast_job(src_refs, dst_refs):
    for src, dst in zip(src_refs, dst_refs):
        dst[...] = src[...].astype(BF16)


def _mod_kernel(c_ref, w_ref, b_ref, o_ref):
    c = c_ref[...]
    c_act = _silu(c).astype(BF16)
    o_ref[...] = _dot(c_act, w_ref[...].astype(BF16)) + b_ref[pl.ds(pl.program_id(0), 1), :]


def _modulation(c, w_mod, b_mod):
    depth, d, n = w_mod.shape
    b = c.shape[0]
    return pl.pallas_call(
        _mod_kernel,
        grid=(depth, n // MOD_COLS),
        in_specs=[
            pl.BlockSpec((b, d), lambda l, j: (0, 0)),
            pl.BlockSpec((None, d, MOD_COLS), lambda l, j: (l, 0, j)),
            pl.BlockSpec((depth, MOD_COLS), lambda l, j: (0, j)),
        ],
        out_specs=pl.BlockSpec((None, b, MOD_COLS), lambda l, j: (l, 0, j)),
        out_shape=jax.ShapeDtypeStruct((depth, b, n), F32),
        compiler_params=pltpu.CompilerParams(
            dimension_semantics=("arbitrary", "arbitrary"), vmem_limit_bytes=VMEM_LIMIT),
        name="modulation",
    )(c, w_mod, b_mod)


def _mix_in_kernel(x_ref, mod_ref, g_ref, win_ref, gsgu_ref, wsgu_ref, bsp_ref,
                   q_ref, k_ref, v_ref, kmean_ref, sgu_ref, *, layer):
    tm = x_ref.shape[0]
    g = g_ref[layer:layer + 1, :]
    h = _norm_mod(x_ref[...], g, mod_ref[0:1, :], mod_ref[1:2, :]).astype(BF16)

    uv = _dot(h, win_ref[:, QKV_WIDTH:QKV_WIDTH + UV_WIDTH])
    qk = _dot(h, win_ref[:, :2 * ATTN_WIDTH])
    q_ref[...] = (qk[:, :ATTN_WIDTH] * Q_SCALE).astype(BF16)
    k = qk[:, ATTN_WIDTH:]
    k_ref[...] = k.astype(BF16)
    for c in range(tm // MOBA_BLOCK):
        kmean_ref[c] = jnp.mean(k[c * MOBA_BLOCK:(c + 1) * MOBA_BLOCK], axis=0, keepdims=True)
    v_mid = 2 * ATTN_WIDTH + ATTN_WIDTH // 2
    v_ref[:, :ATTN_WIDTH // 2] = _dot(h, win_ref[:, 2 * ATTN_WIDTH:v_mid]).astype(BF16)

    gu = _gelu_tanh(uv[:, :SGU_WIDTH])
    vn = _rms(_gelu_tanh(uv[:, SGU_WIDTH:]), gsgu_ref[layer:layer + 1, :]).astype(BF16)

    lane = lax.broadcasted_iota(jnp.int32, (SGU_CHUNK, LANES), 1)
    first = lane < SGU_GROUP_DIM
    wrow = lax.broadcasted_iota(jnp.int32, (SGU_CHUNK, 2 * SGU_CHUNK), 0)
    wcol = lax.broadcasted_iota(jnp.int32, (SGU_CHUNK, 2 * SGU_CHUNK), 1)
    causal = jnp.where(wcol >= SGU_CHUNK, wcol - SGU_CHUNK, wcol) <= wrow
    zero = jnp.zeros((), BF16)
    chunks = [slice(c * SGU_CHUNK, (c + 1) * SGU_CHUNK) for c in range(tm // SGU_CHUNK)]
    for gp in range(SGU_WIDTH // LANES):
        cols = slice(gp * LANES, (gp + 1) * LANES)
        w = jnp.concatenate([wsgu_ref[2 * gp], wsgu_ref[2 * gp + 1]], axis=1)
        w = jnp.where(causal, w, 0.0).astype(BF16)
        bias = bsp_ref[:, cols]
        rhs = jnp.concatenate(
            [jnp.concatenate([jnp.where(first, vn[rows, cols], zero),
                              jnp.where(first, zero, vn[rows, cols])], axis=0)
             for rows in chunks], axis=1)
        mixed = _dot(w, rhs)
        for c, rows in enumerate(chunks):
            sgu_ref[rows, cols] = (
                gu[rows, cols] * (mixed[:, c * LANES:(c + 1) * LANES] + bias)).astype(BF16)

    v_ref[:, ATTN_WIDTH // 2:] = _dot(h, win_ref[:, v_mid:QKV_WIDTH]).astype(BF16)


def _mix_in(layer, x, mod, g_mix, w_in, g_sgu, w_sgu_s, b_sp):
    b, s, d = x.shape
    tm = TOKEN_TILE
    blocks_per_tile = tm // MOBA_BLOCK
    act = lambda width: jax.ShapeDtypeStruct((b, s, width), BF16)
    act_spec = lambda width: pl.BlockSpec((None, tm, width), lambda bi, i: (bi, i, 0))
    return pl.pallas_call(
        functools.partial(_mix_in_kernel, layer=layer),
        grid=(b, s // tm),
        in_specs=[
            pl.BlockSpec((None, tm, d), lambda bi, i: (bi, i, 0)),
            pl.BlockSpec((None, None, N_MOD, d), lambda bi, i: (layer, bi, 0, 0)),
            _full_spec(g_mix.shape),
            pl.BlockSpec((d, QKV_WIDTH + UV_WIDTH), lambda bi, i: (0, 0)),
            _full_spec(g_sgu.shape),
            _layer_spec(w_sgu_s.shape, layer),
            _layer_spec(b_sp.shape, layer),
        ],
        out_specs=[
            act_spec(ATTN_WIDTH), act_spec(ATTN_WIDTH), act_spec(ATTN_WIDTH),
            pl.BlockSpec((None, blocks_per_tile, 1, ATTN_WIDTH), lambda bi, i: (bi, i, 0, 0)),
            act_spec(SGU_WIDTH),
        ],
        out_shape=[
            act(ATTN_WIDTH), act(ATTN_WIDTH), act(ATTN_WIDTH),
            jax.ShapeDtypeStruct((b, s // MOBA_BLOCK, 1, ATTN_WIDTH), F32),
            act(SGU_WIDTH),
        ],
        compiler_params=pltpu.CompilerParams(
            dimension_semantics=("arbitrary", "arbitrary"), vmem_limit_bytes=VMEM_LIMIT),
        name="mix_in",
    )(x, mod, g_mix, w_in, g_sgu, w_sgu_s, b_sp)


def _moba_kernel(*refs, n_cast):
    (q_ref, k_ref, v_ref, km_ref), refs = refs[:4], refs[4:]
    cast_src, refs = refs[:n_cast], refs[n_cast:]
    o_ref, refs = refs[0], refs[1:]
    cast_dst, (vt_ref, ot_ref) = refs[:n_cast], refs[n_cast:]
    _run_cast_job(cast_src, cast_dst)

    nb = km_ref.shape[0]
    s_len = q_ref.shape[0]
    neg_inf = jnp.float32(-jnp.inf)
    blocks = [slice(j * MOBA_BLOCK, (j + 1) * MOBA_BLOCK) for j in range(nb)]

    q = q_ref[...]
    lane = lax.broadcasted_iota(jnp.int32, (s_len, LANES), 1)
    qh = []
    for h in range(HEADS_PER_STEP):
        in_head = (lane >= h * HEAD_DIM) & (lane < (h + 1) * HEAD_DIM)
        qh.append(jnp.where(in_head, q, jnp.zeros((), BF16)))

    def scores(h, i):
        return _dot_nt(k_ref[:(i + 1) * MOBA_BLOCK, :], qh[h][blocks[i]])

    order = [(h, i) for i in range(nb) for h in range(HEADS_PER_STEP)]
    pending = [scores(*order[n]) for n in range(SCORE_LOOKAHEAD)]

    for j in range(nb):
        vt = v_ref[blocks[j], :].astype(F32).T.astype(BF16)
        for h in range(HEADS_PER_STEP):
            vt_ref[h, :HEAD_DIM, blocks[j]] = vt[h * HEAD_DIM:(h + 1) * HEAD_DIM]
    for h in range(HEADS_PER_STEP):
        vt_ref[h, HEAD_DIM:, :] = jnp.ones((BF16_ROWS, s_len), BF16)

    km = km_ref[:, 0, :]
    km_hi = km.astype(BF16).astype(F32)
    km_lane = lax.broadcasted_iota(jnp.int32, (nb, LANES), 1)
    km_parts = []
    for h in range(HEADS_PER_STEP):
        in_head = (km_lane >= h * HEAD_DIM) & (km_lane < (h + 1) * HEAD_DIM)
        km_parts += [jnp.where(in_head, km_hi, 0.0), jnp.where(in_head, km - km_hi, 0.0)]
    route_parts = _dot_nt(jnp.concatenate(km_parts, axis=0).astype(BF16), q)
    blk = lax.broadcasted_iota(jnp.int32, (nb, s_len), 0)
    q_blk = lax.broadcasted_iota(jnp.int32, (nb, s_len), 1) // MOBA_BLOCK
    kpos = lax.broadcasted_iota(jnp.int32, (MOBA_BLOCK, MOBA_BLOCK), 0)
    qpos = lax.broadcasted_iota(jnp.int32, (MOBA_BLOCK, MOBA_BLOCK), 1)
    causal = kpos <= qpos

    bias = []
    for h in range(HEADS_PER_STEP):
        route = (route_parts[2 * h * nb:(2 * h + 1) * nb]
                 + route_parts[(2 * h + 1) * nb:(2 * h + 2) * nb])
        rank = jnp.zeros((nb, s_len), jnp.int32)
        for jp in range(nb):
            rj = route[jp:jp + 1, :]
            beats = (rj > route) | ((rj == route) & (blk > jp))
            rank = rank + jnp.where(beats & (q_blk > jp), 1, 0)
        bias.append(jnp.where((blk < q_blk) & (rank < MOBA_TOPK), 0.0, neg_inf))

    for n, (h, i) in enumerate(order):
        s = pending.pop(0)
        if n + SCORE_LOOKAHEAD < len(order):
            pending.append(scores(*order[n + SCORE_LOOKAHEAD]))
        parts = [s[blocks[j]] + bias[h][j:j + 1, blocks[i]] for j in range(i)]
        parts.append(jnp.where(causal, s[blocks[i]], neg_inf))
        s = jnp.concatenate(parts, axis=0)
        m = jnp.max(s, axis=0, keepdims=True)
        p = jnp.exp2(s - m).astype(BF16)
        pv = _dot(vt_ref[h, :, :(i + 1) * MOBA_BLOCK], p)
        ot_ref[h * HEAD_DIM:(h + 1) * HEAD_DIM, blocks[i]] = (
            pv[:HEAD_DIM] / pv[HEAD_DIM:HEAD_DIM + 1])
        if h == HEADS_PER_STEP - 1:
            o_ref[blocks[i], :] = ot_ref[:, blocks[i]].T.astype(o_ref.dtype)


def _moba(layer, q, k, v, kmean, cast_weights):
    b, s, width = q.shape
    nb = s // MOBA_BLOCK
    grid = (b, width // LANES)
    slab = pl.BlockSpec((None, s, LANES), lambda bi, p: (bi, 0, p))
    cast_in, cast_out, cast_shapes = _cast_job(cast_weights, layer, grid)
    outs = pl.pallas_call(
        functools.partial(_moba_kernel, n_cast=len(cast_weights)),
        grid=grid,
        in_specs=[slab, slab, slab,
                  pl.BlockSpec((None, nb, 1, LANES), lambda bi, p: (bi, 0, 0, p))] + cast_in,
        out_specs=[slab] + cast_out,
        out_shape=[jax.ShapeDtypeStruct((b, s, width), BF16)] + cast_shapes,
        scratch_shapes=[
            pltpu.VMEM((HEADS_PER_STEP, HEAD_DIM + BF16_ROWS, s), BF16),
            pltpu.VMEM((LANES, s), F32),
        ],
        compiler_params=pltpu.CompilerParams(
            dimension_semantics=("arbitrary", "arbitrary"), vmem_limit_bytes=VMEM_LIMIT),
        name="moba",
    )(q, k, v, kmean, *cast_weights)
    return outs[0], tuple(outs[1:])


def _mix_out_kernel(x_ref, mod_ref, g_ref, attn_ref, sgu_ref, win_ref, wa_ref, ws_ref, wo_ref,
                    o_ref, *, layer):
    d = x_ref.shape[1]
    gate_cols = QKV_WIDTH + UV_WIDTH
    ya = _dot(attn_ref[...], wa_ref[...])
    ys = _dot(sgu_ref[...], ws_ref[...])
    x = x_ref[...]
    h = _norm_mod(x, g_ref[layer:layer + 1, :], mod_ref[0:1, :], mod_ref[1:2, :]).astype(BF16)
    gates = _dot(h, win_ref[:, gate_cols:])
    merged = _sigmoid(gates[:, :d]) * ya + _sigmoid(gates[:, d:]) * ys
    o_ref[...] = x + mod_ref[2:3, :] * _dot(merged.astype(BF16), wo_ref[...])


def _mix_out(layer, x, mod, g_mix, attn, sgu, w_in, w_attn_br, w_sgu_br, w_out):
    b, s, d = x.shape
    tm = TOKEN_TILE
    tile = lambda width: pl.BlockSpec((None, tm, width), lambda bi, i: (bi, i, 0))
    return pl.pallas_call(
        functools.partial(_mix_out_kernel, layer=layer),
        grid=(b, s // tm),
        in_specs=[
            tile(d),
            pl.BlockSpec((None, None, N_MOD, d), lambda bi, i: (layer, bi, 0, 0)),
            _full_spec(g_mix.shape),
            tile(attn.shape[2]),
            tile(sgu.shape[2]),
            _full_spec(w_in.shape),
            _full_spec(w_attn_br.shape),
            _full_spec(w_sgu_br.shape),
            _full_spec(w_out.shape),
        ],
        out_specs=tile(d),
        out_shape=jax.ShapeDtypeStruct(x.shape, F32),
        compiler_params=pltpu.CompilerParams(
            dimension_semantics=("arbitrary", "arbitrary"), vmem_limit_bytes=VMEM_LIMIT),
        name="mix_out",
    )(x, mod, g_mix, attn, sgu, w_in, w_attn_br, w_sgu_br, w_out)


def _conv_ffn_kernel(*refs, layer, final_norm, n_cast):
    (x_ref, mod_ref, g_ref, wup_ref, taps_ref, wdown_ref, gfin_ref), refs = refs[:7], refs[7:]
    cast_src, refs = refs[:n_cast], refs[n_cast:]
    o_ref, refs = refs[0], refs[1:]
    cast_dst, refs = refs[:n_cast], refs[n_cast:]
    h_ref, gated_ref, tail_ref, wina_ref, winl_ref = refs

    i = pl.program_id(1)
    tm = x_ref.shape[0]
    d_ff = wdown_ref.shape[0]
    tf = FF_CHUNK
    x = x_ref[...]
    g = g_ref[layer:layer + 1, :]
    h_ref[...] = _norm_mod(x, g, mod_ref[3:4, :], mod_ref[4:5, :]).astype(BF16)

    _run_cast_job(cast_src, cast_dst)

    def conv_chunk(cols, win_ref):
        up = _dot(h_ref[...], wup_ref[:, cols])
        win_ref[:SUBLANES, :] = jnp.where(i > 0, tail_ref[:, cols], 0.0)
        win_ref[SUBLANES:, :] = up
        tail_ref[:, cols] = up[tm - SUBLANES:, :]
        taps = taps_ref[:, cols]
        conv = taps[3:4, :] + taps[2:3, :] * up
        for back in (1, 2):
            conv = conv + taps[2 - back:3 - back, :] * win_ref[pl.ds(SUBLANES - back, tm), :]
        return conv

    for j in range(d_ff // tf):
        act = conv_chunk(slice(j * tf, (j + 1) * tf), wina_ref)
        lin = conv_chunk(slice(d_ff + j * tf, d_ff + (j + 1) * tf), winl_ref)
        gated_ref[:, j * tf:(j + 1) * tf] = ((act + act * jnp.tanh(act)) * lin).astype(BF16)

    out = x + mod_ref[5:6, :] * _dot(gated_ref[...], wdown_ref[...])
    if final_norm:
        out = _rms(out, gfin_ref[...])
    o_ref[...] = out


def _conv_ffn(layer, x, mod, g_ffn, w_up, taps, w_down, g_final, final_norm, next_weights):
    b, s, d = x.shape
    tm = TOKEN_TILE
    grid = (b, s // tm)
    d_ff = w_down.shape[0]
    tile = pl.BlockSpec((None, tm, d), lambda bi, i: (bi, i, 0))
    cast_in, cast_out, cast_shapes = _cast_job(next_weights, layer + 1, grid)
    outs = pl.pallas_call(
        functools.partial(_conv_ffn_kernel, layer=layer, final_norm=final_norm,
                          n_cast=len(next_weights)),
        grid=grid,
        in_specs=[
            tile,
            pl.BlockSpec((None, None, N_MOD, d), lambda bi, i: (layer, bi, 0, 0)),
            _full_spec(g_ffn.shape),
            _full_spec(w_up.shape),
            _layer_spec(taps.shape, layer),
            _full_spec(w_down.shape),
            _full_spec(g_final.shape),
        ] + cast_in,
        out_specs=[tile] + cast_out,
        out_shape=[jax.ShapeDtypeStruct(x.shape, F32)] + cast_shapes,
        scratch_shapes=[
            pltpu.VMEM((tm, d), BF16),
            pltpu.VMEM((tm, d_ff), BF16),
            pltpu.VMEM((SUBLANES, 2 * d_ff), F32),
            pltpu.VMEM((SUBLANES + tm, FF_CHUNK), F32),
            pltpu.VMEM((SUBLANES + tm, FF_CHUNK), F32),
        ],
        compiler_params=pltpu.CompilerParams(
            dimension_semantics=("arbitrary", "arbitrary"), vmem_limit_bytes=VMEM_LIMIT),
        name="conv_ffn",
    )(x, mod, g_ffn, w_up, taps, w_down, g_final, *next_weights)
    return outs[0], tuple(outs[1:])


def _mix_ffn_kernel(*refs, layer, final_norm, n_cast):
    (x_ref, mod_ref, gmix_ref, attn_ref, sgu_ref, win_ref, wa_ref, ws_ref, wo_ref,
     gffn_ref, wup_ref, taps_ref, wdown_ref, gfin_ref), refs = refs[:14], refs[14:]
    cast_src, refs = refs[:n_cast], refs[n_cast:]
    o_ref, refs = refs[0], refs[1:]
    cast_dst, refs = refs[:n_cast], refs[n_cast:]
    h_ref, gated_ref, tail_ref, wina_ref, winl_ref = refs

    i = pl.program_id(1)
    tm, d = x_ref.shape
    d_ff = wdown_ref.shape[0]
    tf = FF_CHUNK
    gate_cols = QKV_WIDTH + UV_WIDTH

    ya = _dot(attn_ref[...], wa_ref[...])
    ys = _dot(sgu_ref[...], ws_ref[...])
    x0 = x_ref[...]
    h = _norm_mod(x0, gmix_ref[layer:layer + 1, :], mod_ref[0:1, :], mod_ref[1:2, :]).astype(BF16)
    gates = _dot(h, win_ref[:, gate_cols:])
    merged = _sigmoid(gates[:, :d]) * ya + _sigmoid(gates[:, d:]) * ys
    x = x0 + mod_ref[2:3, :] * _dot(merged.astype(BF16), wo_ref[...])

    g = gffn_ref[layer:layer + 1, :]
    h_ref[...] = _norm_mod(x, g, mod_ref[3:4, :], mod_ref[4:5, :]).astype(BF16)
    _run_cast_job(cast_src, cast_dst)

    def conv_chunk(cols, win_ref):
        up = _dot(h_ref[...], wup_ref[:, cols])
        win_ref[:SUBLANES, :] = jnp.where(i > 0, tail_ref[:, cols], 0.0)
        win_ref[SUBLANES:, :] = up
        tail_ref[:, cols] = up[tm - SUBLANES:, :]
        taps = taps_ref[:, cols]
        conv = taps[3:4, :] + taps[2:3, :] * up
        for back in (1, 2):
            conv = conv + taps[2 - back:3 - back, :] * win_ref[pl.ds(SUBLANES - back, tm), :]
        return conv

    for j in range(d_ff // tf):
        act = conv_chunk(slice(j * tf, (j + 1) * tf), wina_ref)
        lin = conv_chunk(slice(d_ff + j * tf, d_ff + (j + 1) * tf), winl_ref)
        gated_ref[:, j * tf:(j + 1) * tf] = ((act + act * jnp.tanh(act)) * lin).astype(BF16)

    out = x + mod_ref[5:6, :] * _dot(gated_ref[...], wdown_ref[...])
    if final_norm:
        out = _rms(out, gfin_ref[...])
    o_ref[...] = out


def _mix_ffn(layer, x, mod, g_mix, attn, sgu, w_in, w_attn_br, w_sgu_br, w_out, g_ffn, w_up, taps,
             w_down, g_final, final_norm, next_weights):
    b, s, d = x.shape
    tm = FUSED_TILE
    grid = (b, s // tm)
    d_ff = w_down.shape[0]
    tile = lambda width: pl.BlockSpec((None, tm, width), lambda bi, i: (bi, i, 0))
    cast_in, cast_out, cast_shapes = _cast_job(next_weights, layer + 1, grid)
    outs = pl.pallas_call(
        functools.partial(_mix_ffn_kernel, layer=layer, final_norm=final_norm,
                          n_cast=len(next_weights)),
        grid=grid,
        in_specs=[
            tile(d),
            pl.BlockSpec((None, None, N_MOD, d), lambda bi, i: (layer, bi, 0, 0)),
            _full_spec(g_mix.shape),
            tile(attn.shape[2]),
            tile(sgu.shape[2]),
            _full_spec(w_in.shape),
            _full_spec(w_attn_br.shape),
            _full_spec(w_sgu_br.shape),
            _full_spec(w_out.shape),
            _full_spec(g_ffn.shape),
            _full_spec(w_up.shape),
            _layer_spec(taps.shape, layer),
            _full_spec(w_down.shape),
            _full_spec(g_final.shape),
        ] + cast_in,
        out_specs=[tile(d)] + cast_out,
        out_shape=[jax.ShapeDtypeStruct(x.shape, F32)] + cast_shapes,
        scratch_shapes=[
            pltpu.VMEM((tm, d), BF16),
            pltpu.VMEM((tm, d_ff), BF16),
            pltpu.VMEM((SUBLANES, 2 * d_ff), F32),
            pltpu.VMEM((SUBLANES + tm, FF_CHUNK), F32),
            pltpu.VMEM((SUBLANES + tm, FF_CHUNK), F32),
        ],
        compiler_params=pltpu.CompilerParams(
            dimension_semantics=("arbitrary", "arbitrary"), vmem_limit_bytes=VMEM_LIMIT),
        input_output_aliases={0: 0},
        name="mix_ffn",
    )(x, mod, g_mix, attn, sgu, w_in, w_attn_br, w_sgu_br, w_out, g_ffn, w_up, taps, w_down,
      g_final, *next_weights)
    return outs[0], tuple(outs[1:])


def kernel(x, c, w_mod, b_mod, g_mix, w_in, g_sgu, w_sgu_s, b_sgu_s, w_attn_br, w_sgu_br, w_out,
           g_ffn, w_up, w_conv, b_conv, w_down, g_final):
    b, s, d = x.shape
    depth = w_mod.shape[0]
    d_ff = w_down.shape[1]
    assert s % TOKEN_TILE == 0 and TOKEN_TILE % MOBA_BLOCK == 0 and d_ff % FF_CHUNK == 0

    later_weights = (w_attn_br, w_sgu_br, w_out, w_up, w_down)
    w_in_b = w_in[0].astype(BF16)
    b_sp = jnp.repeat(jnp.swapaxes(b_sgu_s, 1, 2), SGU_GROUP_DIM, axis=2)
    taps = jnp.concatenate(
        [w_conv, b_conv[:, None, :],
         jnp.zeros((depth, SUBLANES - CONV_WIDTH - 1, 2 * d_ff), F32)], axis=1)
    taps = taps * jnp.where(jnp.arange(2 * d_ff) < d_ff, 0.5, 1.0).astype(F32)
    g_fin = g_final.reshape(1, d)

    mod = _modulation(c, w_mod, b_mod).reshape(depth, b, N_MOD, d)
    for l in range(depth):
        last = l == depth - 1
        q, k, v, kmean, sgu = _mix_in(l, x, mod, g_mix, w_in_b, g_sgu, w_sgu_s, b_sp)
        attn, (w_attn_b, w_sgu_b, w_out_b, w_up_b, w_down_b) = _moba(
            l, q, k, v, kmean, later_weights)
        x, next_w_in = _mix_ffn(l, x, mod, g_mix, attn, sgu, w_in_b, w_attn_b, w_sgu_b, w_out_b,
                                g_ffn, w_up_b, taps, w_down_b, g_fin,
                                final_norm=last, next_weights=() if last else (w_in,))
        if not last:
            (w_in_b,) = next_w_in
    return x
```
